```python
import math
import jax, jax.numpy as jnp
from jax import lax
import numpy as np

D_MODEL = 2048
BATCH = 4
SEQ = 2048
DEPTH = 4
DEC_BATCH = 8
DEC_SEQ = 2048
PAST_LEN = 128

N_MIXERS = 2
N_POOL_LAYERS = (DEPTH + 1) // 2
N_GMLP_LAYERS = DEPTH // 2
POOL_WINDOWS = (2, 4, 8, 16)
N_POOL_GROUPS = len(POOL_WINDOWS)
POOL_GROUP_DIM = D_MODEL // N_POOL_GROUPS
CHUNK = 128
D_GATE = D_MODEL
N_SG_HEADS = 8
SG_HEAD_DIM = D_GATE // N_SG_HEADS
N_MEM = 256
N_XHEADS = 4
XHEAD_DIM = D_MODEL // N_XHEADS
D_FF = int(math.ceil((8 * D_MODEL / 3) / 256) * 256)
N_NORMS = 6
EPS = 1e-6

kernel_name = "hybrid_pool_gmlp_encoder"


def rmsnorm(x, g):
    xf = x.astype(jnp.float32)
    r = xf * lax.rsqrt(jnp.mean(xf * xf, axis=-1, keepdims=True) + EPS)
    return (r * g.astype(jnp.float32)).astype(x.dtype)


def layernorm(x, g, b):
    xf = x.astype(jnp.float32)
    mu = jnp.mean(xf, axis=-1, keepdims=True)
    xc = xf - mu
    var = jnp.mean(xc * xc, axis=-1, keepdims=True)
    y = xc * lax.rsqrt(var + EPS) * g.astype(jnp.float32) + b.astype(jnp.float32)
    return y.astype(x.dtype)


def pool_mixer(x, w_group, scale):
    B, S, D = x.shape
    xf = x.astype(jnp.float32)
    c = jnp.concatenate([jnp.zeros((B, 1, D), jnp.float32), jnp.cumsum(xf, axis=1)], axis=1)
    c = c.reshape(B, S + 1, N_POOL_GROUPS, POOL_GROUP_DIM)
    t = jnp.arange(S, dtype=jnp.int32)[:, None]
    half = jnp.array(POOL_WINDOWS, dtype=jnp.int32)[None, :] // 2
    lo = jnp.clip(t - half, 0, S)
    hi = jnp.clip(t + half, 0, S)
    gidx = jnp.arange(N_POOL_GROUPS, dtype=jnp.int32)[None, :]
    win_sum = c[:, hi, gidx, :] - c[:, lo, gidx, :]
    count = (hi - lo).astype(jnp.float32)[None, :, :, None]
    diff = win_sum / count - xf.reshape(B, S, N_POOL_GROUPS, POOL_GROUP_DIM)
    y = jnp.einsum('bsgc,gcd->bsgd', diff.astype(x.dtype), w_group)
    return y.reshape(B, S, D) * scale


def gmlp_mixer(x, w_in, ln_g, ln_b, w_s, b_s, w_out):
    B, S, D = x.shape
    h = jax.nn.gelu(x @ w_in)
    u, v = h[..., :D_GATE], h[..., D_GATE:]
    v = layernorm(v, ln_g, ln_b)
    v = v.reshape(B, S // CHUNK, CHUNK, N_SG_HEADS, SG_HEAD_DIM)
    mixed = jnp.einsum('hpq,bnqhc->bnphc', w_s, v) + b_s.T[None, None, :, :, None]
    gated = u * mixed.reshape(B, S, D_GATE)
    return gated @ w_out


def cross_attention(x, mem, wq, wk, wv, wo):
    B, S, D = x.shape
    M = mem.shape[1]
    q = (x @ wq).reshape(B, S, N_XHEADS, XHEAD_DIM)
    k = (mem @ wk).reshape(B, M, N_XHEADS, XHEAD_DIM)
    v = (mem @ wv).reshape(B, M, N_XHEADS, XHEAD_DIM)
    s = jnp.einsum('bshd,bmhd->bhsm', q, k).astype(jnp.float32) * (XHEAD_DIM ** -0.5)
    p = jax.nn.softmax(s, axis=-1).astype(x.dtype)
    o = jnp.einsum('bhsm,bmhd->bshd', p, v).reshape(B, S, D)
    return o @ wo


def swiglu(x, w_gate, w_up, w_down):
    return (jax.nn.silu(x @ w_gate) * (x @ w_up)) @ w_down


def trunk(x, mem, norm_gains, mem_norm, pool_w, pool_scale, gmlp_w_in, gmlp_ln_g, gmlp_ln_b,
          gmlp_w_s, gmlp_b_s, gmlp_w_out, attn_wq, attn_wk, attn_wv, attn_wo,
          ffn_w_gate, ffn_w_up, ffn_w_down):
    for i in range(DEPTH):
        g = norm_gains[i]
        j = i // N_MIXERS
        h = rmsnorm(x, g[0])
        if i % N_MIXERS == 0:
            h = pool_mixer(h, pool_w[j], pool_scale[j])
        else:
            h = gmlp_mixer(h, gmlp_w_in[j], gmlp_ln_g[j], gmlp_ln_b[j], gmlp_w_s[j],
                           gmlp_b_s[j], gmlp_w_out[j])
        x = x + rmsnorm(h, g[1])
        memn = rmsnorm(mem, mem_norm[i])
        h = cross_attention(rmsnorm(x, g[2]), memn, attn_wq[i], attn_wk[i], attn_wv[i], attn_wo[i])
        x = x + rmsnorm(h, g[3])
        h = swiglu(rmsnorm(x, g[4]), ffn_w_gate[i], ffn_w_up[i], ffn_w_down[i])
        x = x + rmsnorm(h, g[5])
    return x


def setup_inputs(seed: int = 0) -> dict:
    key = jax.random.key(seed)
    ks = jax.random.split(key, 24)
    f32 = jnp.float32

    def nrm(k, shape, scale):
        return jax.random.normal(k, shape, f32) * scale

    return {
        "x_prompt": nrm(ks[0], (BATCH, SEQ, D_MODEL), 1.0),
        "x_sample": nrm(ks[1], (DEC_BATCH, DEC_SEQ, D_MODEL), 1.0),
        "mem_prompt": nrm(ks[2], (BATCH, N_MEM, D_MODEL), 1.0),
        "mem_sample": nrm(ks[3], (DEC_BATCH, N_MEM, D_MODEL), 1.0),
        "norm_gains": 1.0 + nrm(ks[4], (DEPTH, N_NORMS, D_MODEL), 0.05),
        "mem_norm": 1.0 + nrm(ks[5], (DEPTH, D_MODEL), 0.05),
        "pool_w": nrm(ks[6], (N_POOL_LAYERS, N_POOL_GROUPS, POOL_GROUP_DIM, POOL_GROUP_DIM), POOL_GROUP_DIM ** -0.5),
        "pool_scale": 1.0 + nrm(ks[7], (N_POOL_LAYERS, D_MODEL), 0.1),
        "gmlp_w_in": nrm(ks[8], (N_GMLP_LAYERS, D_MODEL, 2 * D_GATE), D_MODEL ** -0.5),
        "gmlp_ln_g": 1.0 + nrm(ks[9], (N_GMLP_LAYERS, D_GATE), 0.05),
        "gmlp_ln_b": nrm(ks[10], (N_GMLP_LAYERS, D_GATE), 0.02),
        "gmlp_w_s": nrm(ks[11], (N_GMLP_LAYERS, N_SG_HEADS, CHUNK, CHUNK), 0.5 * CHUNK ** -0.5),
        "gmlp_b_s": 1.0 + nrm(ks[12], (N_GMLP_LAYERS, N_SG_HEADS, CHUNK), 0.1),
        "gmlp_w_out": nrm(ks[13], (N_GMLP_LAYERS, D_GATE, D_MODEL), D_GATE ** -0.5),
        "attn_wq": nrm(ks[14], (DEPTH, D_MODEL, D_MODEL), D_MODEL ** -0.5),
        "attn_wk": nrm(ks[15], (DEPTH, D_MODEL, D_MODEL), D_MODEL ** -0.5),
        "attn_wv": nrm(ks[16], (DEPTH, D_MODEL, D_MODEL), D_MODEL ** -0.5),
        "attn_wo": nrm(ks[17], (DEPTH, D_MODEL, D_MODEL), D_MODEL ** -0.5),
        "ffn_w_gate": nrm(ks[18], (DEPTH, D_MODEL, D_FF), D_MODEL ** -0.5),
        "ffn_w_up": nrm(ks[19], (DEPTH, D_MODEL, D_FF), D_MODEL ** -0.5),
        "ffn_w_down": nrm(ks[20], (DEPTH, D_FF, D_MODEL), D_FF ** -0.5),
    }


def reference(x_prompt, x_sample, mem_prompt, mem_sample, norm_gains, mem_norm, pool_w, pool_scale,
              gmlp_w_in, gmlp_ln_g, gmlp_ln_b, gmlp_w_s, gmlp_b_s, gmlp_w_out,
              attn_wq, attn_wk, attn_wv, attn_wo, ffn_w_gate, ffn_w_up, ffn_w_down):
    y_prompt = trunk(x_prompt, mem_prompt, norm_gains, mem_norm, pool_w, pool_scale, gmlp_w_in,
                     gmlp_ln_g, gmlp_ln_b, gmlp_w_s, gmlp_b_s, gmlp_w_out, attn_wq, attn_wk,
                     attn_wv, attn_wo, ffn_w_gate, ffn_w_up, ffn_w_down)
    y_sample = trunk(x_sample, mem_sample, norm_gains, mem_norm, pool_w, pool_scale, gmlp_w_in,
                     gmlp_ln_g, gmlp_ln_b, gmlp_w_s, gmlp_b_s, gmlp_w_out, attn_wq, attn_wk,
                     attn_wv, attn_wo, ffn_w_gate, ffn_w_up, ffn_w_down)
    return (y_prompt, y_sample)
```

```python
import functools

import numpy as np
import jax
import jax.numpy as jnp
from jax import lax
from jax.experimental import pallas as pl
from jax.experimental.pallas import tpu as pltpu

F32 = jnp.float32
BF16 = jnp.bfloat16

EPS = 1e-6
POOL_WINDOWS = (2, 4, 8, 16)
POOL_HALO = max(POOL_WINDOWS) // 2
CHUNK = 128
N_SG_HEADS = 8
N_XHEADS = 4

V7X_SUBLANES = 8
V7X_VMEM_LIMIT_BYTES = 56 * 1024 * 1024

GELU_C0 = np.float32(np.sqrt(2.0 / np.pi))
GELU_C1 = np.float32(0.044715)


def _rms(x, g):
    ms = jnp.mean(x * x, axis=-1, keepdims=True)
    return x * lax.rsqrt(ms + EPS) * g


def _gelu_tanh(x):
    return x * (0.5 * (1.0 + jnp.tanh(GELU_C0 * (x + GELU_C1 * (x * x * x)))))


def _dot(a, b):
    return jnp.dot(a, b, preferred_element_type=F32)


def _const_spec(shape):
    return pl.BlockSpec(shape, lambda *_: (0,) * len(shape), pipeline_mode=pl.Buffered(1))


def _params(semantics):
    return pltpu.CompilerParams(dimension_semantics=semantics,
                                vmem_limit_bytes=V7X_VMEM_LIMIT_BYTES)


def _ffn_kernel(x_ref, gpre_ref, gpost_ref, wg_ref, wu_ref, wd_ref, o_ref, xn_ref, acc_ref):
    k = pl.program_id(1)

    @pl.when(k == 0)
    def _():
        xn_ref[...] = _rms(x_ref[...], gpre_ref[...]).astype(BF16)

    xn = xn_ref[...]
    gate = _dot(xn, wg_ref[...])
    up = _dot(xn, wu_ref[...])
    h = (gate * (1.0 / (1.0 + jnp.exp(-gate))) * up).astype(BF16)
    part = _dot(h, wd_ref[...])

    @pl.when(k == 0)
    def _():
        acc_ref[...] = part

    @pl.when(k > 0)
    def _():
        acc_ref[...] += part

    @pl.when(k == pl.num_programs(1) - 1)
    def _():
        o_ref[...] = x_ref[...] + _rms(acc_ref[...], gpost_ref[...])


def _ffn(x, gpre, gpost, wg, wu, wd, *, tm, tf):
    t, d = x.shape
    f = wg.shape[1]
    return pl.pallas_call(
        _ffn_kernel,
        grid=(t // tm, f // tf),
        in_specs=[
            pl.BlockSpec((tm, d), lambda i, k: (i, 0)),
            pl.BlockSpec((1, d), lambda i, k: (0, 0)),
            pl.BlockSpec((1, d), lambda i, k: (0, 0)),
            pl.BlockSpec((d, tf), lambda i, k: (0, k)),
            pl.BlockSpec((d, tf), lambda i, k: (0, k)),
            pl.BlockSpec((tf, d), lambda i, k: (k, 0)),
        ],
        out_specs=pl.BlockSpec((tm, d), lambda i, k: (i, 0)),
        out_shape=jax.ShapeDtypeStruct((t, d), F32),
        scratch_shapes=[pltpu.VMEM((tm, d), BF16), pltpu.VMEM((tm, d), F32)],
        compiler_params=_params(("parallel", "arbitrary")),
        name="ffn",
    )(x, gpre, gpost, wg, wu, wd)


def _kv_kernel(mem_ref, g_ref, wk_ref, wv_ref, kt_ref, v_ref):
    memn = _rms(mem_ref[0], g_ref[0]).astype(BF16)
    kt_ref[0, 0] = _dot(memn, wk_ref[0]).T.astype(BF16)
    v_ref[0, 0] = _dot(memn, wv_ref[0]).astype(BF16)


def _kv(mem, mem_norm, wk, wv):
    b, m, d = mem.shape
    n_layers = wk.shape[0]
    return pl.pallas_call(
        _kv_kernel,
        grid=(n_layers, b),
        in_specs=[
            pl.BlockSpec((1, m, d), lambda l, i: (i, 0, 0)),
            pl.BlockSpec((1, 1, d), lambda l, i: (l, 0, 0)),
            pl.BlockSpec((1, d, d), lambda l, i: (l, 0, 0)),
            pl.BlockSpec((1, d, d), lambda l, i: (l, 0, 0)),
        ],
        out_specs=[
            pl.BlockSpec((1, 1, d, m), lambda l, i: (l, i, 0, 0)),
            pl.BlockSpec((1, 1, m, d), lambda l, i: (l, i, 0, 0)),
        ],
        out_shape=[
            jax.ShapeDtypeStruct((n_layers, b, d, m), BF16),
            jax.ShapeDtypeStruct((n_layers, b, m, d), BF16),
        ],
        compiler_params=_params(("arbitrary", "parallel")),
        name="memory_kv",
    )(mem, mem_norm, wk, wv)


def _attn_kernel(x_ref, gpre_ref, gpost_ref, wq_ref, kt_ref, v_ref, wo_ref, o_ref):
    x = x_ref[0]
    d = x.shape[-1]
    hd = d // N_XHEADS
    xn = _rms(x, gpre_ref[...]).astype(BF16)
    q = _dot(xn, wq_ref[...]).astype(BF16)
    heads = []
    for h in range(N_XHEADS):
        cols = slice(h * hd, (h + 1) * hd)
        s = _dot(q[:, cols], kt_ref[0, 0, cols, :]) * (hd ** -0.5)
        s = s - jnp.max(s, axis=-1, keepdims=True)
        e = jnp.exp(s)
        p = (e / jnp.sum(e, axis=-1, keepdims=True)).astype(BF16)
        heads.append(_dot(p, v_ref[0, 0, :, cols]).astype(BF16))
    o = jnp.concatenate(heads, axis=-1)
    o_ref[0] = x + _rms(_dot(o, wo_ref[...]), gpost_ref[...])


def _attn(x, gpre, gpost, wq, kt, v, wo, layer, *, tm):
    b, s, d = x.shape
    m = v.shape[2]
    return pl.pallas_call(
        _attn_kernel,
        grid=(b, s // tm),
        in_specs=[
            pl.BlockSpec((1, tm, d), lambda i, j: (i, j, 0)),
            _const_spec((1, d)),
            _const_spec((1, d)),
            _const_spec((d, d)),
            pl.BlockSpec((1, 1, d, m), lambda i, j: (layer, i, 0, 0)),
            pl.BlockSpec((1, 1, m, d), lambda i, j: (layer, i, 0, 0)),
            _const_spec((d, d)),
        ],
        out_specs=pl.BlockSpec((1, tm, d), lambda i, j: (i, j, 0)),
        out_shape=jax.ShapeDtypeStruct((b, s, d), F32),
        compiler_params=_params(("parallel", "parallel")),
        name="cross_attn",
    )(x, gpre, gpost, wq, kt, v, wo)


def _pool_kernel(xprev_ref, x_ref, xnext_ref, gpre_ref, gpost_ref, w_ref, scale_ref, o_ref,
                 ext_ref, *, seq_len):
    j = pl.program_id(1)
    tm, d = x_ref.shape[1], x_ref.shape[2]
    gd = d // len(POOL_WINDOWS)
    gpre = gpre_ref[...]
    x = x_ref[0]
    hprev = _rms(xprev_ref[0], gpre)
    hnext = _rms(xnext_ref[0], gpre)
    ext_ref[0:POOL_HALO, :] = jnp.where(j > 0, hprev, 0.0)
    ext_ref[POOL_HALO:POOL_HALO + tm, :] = _rms(x, gpre)
    ext_ref[POOL_HALO + tm:, :] = jnp.where(j < pl.num_programs(1) - 1, hnext, 0.0)

    t = j * tm + lax.broadcasted_iota(jnp.int32, (tm, 1), 0)
    outs = []
    for g, win in enumerate(POOL_WINDOWS):
        half = win // 2
        cols = slice(g * gd, (g + 1) * gd)
        base = POOL_HALO - half
        wsum = ext_ref[base:base + tm, cols]
        for r in range(1, win):
            wsum = wsum + ext_ref[base + r:base + r + tm, cols]
        count = (jnp.minimum(t + half, seq_len) - jnp.maximum(t - half, 0)).astype(F32)
        diff = wsum / count - ext_ref[POOL_HALO:POOL_HALO + tm, cols]
        outs.append(_dot(diff.astype(BF16), w_ref[g]))
    y = jnp.concatenate(outs, axis=-1) * scale_ref[...]
    o_ref[0] = x + _rms(y, gpost_ref[...])


def _pool(x, gpre, gpost, w, scale, *, tm):
    b, s, d = x.shape
    n_groups, gd, _ = w.shape
    halo_blocks_per_tile = tm // POOL_HALO
    n_halo_blocks = s // POOL_HALO
    return pl.pallas_call(
        functools.partial(_pool_kernel, seq_len=s),
        grid=(b, s // tm),
        in_specs=[
            pl.BlockSpec((1, POOL_HALO, d),
                         lambda i, j: (i, jnp.maximum(j * halo_blocks_per_tile - 1, 0), 0)),
            pl.BlockSpec((1, tm, d), lambda i, j: (i, j, 0)),
            pl.BlockSpec((1, POOL_HALO, d),
                         lambda i, j: (i, jnp.minimum((j + 1) * halo_blocks_per_tile,
                                                      n_halo_blocks - 1), 0)),
            _const_spec((1, d)),
            _const_spec((1, d)),
            _const_spec((n_groups, gd, gd)),
            _const_spec((1, d)),
        ],
        out_specs=pl.BlockSpec((1, tm, d), lambda i, j: (i, j, 0)),
        out_shape=jax.ShapeDtypeStruct((b, s, d), F32),
        scratch_shapes=[pltpu.VMEM((tm + 2 * POOL_HALO, d), F32)],
        compiler_params=_params(("parallel", "parallel")),
        name="pool_mixer",
    )(x, x, x, gpre, gpost, w, scale)


def _gmlp_kernel(x_ref, gpre_ref, gpost_ref, wu_ref, wv_ref, lng_ref, lnb_ref, ws_ref, bs_ref,
                 wout_ref, o_ref):
    x = x_ref[0]
    tm, d = x.shape
    hd = d // N_SG_HEADS
    xn = _rms(x, gpre_ref[...]).astype(BF16)

    v = _gelu_tanh(_dot(xn, wv_ref[...]))
    mu = jnp.mean(v, axis=-1, keepdims=True)
    vc = v - mu
    var = jnp.mean(vc * vc, axis=-1, keepdims=True)
    vn = (vc * lax.rsqrt(var + EPS) * lng_ref[...] + lnb_ref[...]).astype(BF16)

    rows = []
    for c in range(tm // CHUNK):
        r = slice(c * CHUNK, (c + 1) * CHUNK)
        rows.append(jnp.concatenate(
            [_dot(ws_ref[h], vn[r, h * hd:(h + 1) * hd]) for h in range(N_SG_HEADS)], axis=-1)
            + bs_ref[...])
    mixed = jnp.concatenate(rows, axis=0)

    u = _gelu_tanh(_dot(xn, wu_ref[...]))
    gated = (u * mixed).astype(BF16)
    o_ref[0] = x + _rms(_dot(gated, wout_ref[...]), gpost_ref[...])


def _gmlp(x, gpre, gpost, wu, wv, lng, lnb, ws, bs_full, wout, *, tm):
    b, s, d = x.shape
    dg = wu.shape[1]
    return pl.pallas_call(
        _gmlp_kernel,
        grid=(b, s // tm),
        in_specs=[
            pl.BlockSpec((1, tm, d), lambda i, j: (i, j, 0)),
            _const_spec((1, d)),
            _const_spec((1, d)),
            _const_spec((d, dg)),
            _const_spec((d, dg)),
            _const_spec((1, dg)),
            _const_spec((1, dg)),
            _const_spec(ws.shape),
            _const_spec(bs_full.shape),
            _const_spec((dg, d)),
        ],
        out_specs=pl.BlockSpec((1, tm, d), lambda i, j: (i, j, 0)),
        out_shape=jax.ShapeDtypeStruct((b, s, d), F32),
        compiler_params=_params(("parallel", "parallel")),
        name="gmlp_mixer",
    )(x, gpre, gpost, wu, wv, lng, lnb, ws, bs_full, wout)


def _trunk(x, kt, v, p):
    b, s, d = x.shape
    depth = p["norm_gains"].shape[0]
    for i in range(depth):
        g = p["norm_gains"][i]
        gain = lambda n: g[n][None, :]
        j = i // 2
        if i % 2 == 0:
            x = _pool(x, gain(0), gain(1), p["pool_w"][j], p["pool_scale"][j][None, :], tm=512)
        else:
            x = _gmlp(x, gain(0), gain(1), p["gmlp_wu"][j], p["gmlp_wv"][j],
                      p["gmlp_ln_g"][j][None, :], p["gmlp_ln_b"][j][None, :], p["gmlp_w_s"][j],
                      p["gmlp_bs_full"][j], p["gmlp_w_out"][j], tm=256)
        x = _attn(x, gain(2), gain(3), p["attn_wq"][i], kt, v, p["attn_wo"][i], i, tm=512)
        x = _ffn(x.reshape(b * s, d), gain(4), gain(5), p["ffn_w_gate"][i], p["ffn_w_up"][i],
                 p["ffn_w_down"][i], tm=512, tf=512).reshape(b, s, d)
    return x


def kernel(x_prompt, x_sample, mem_prompt, mem_sample, norm_gains, mem_norm, pool_w, pool_scale, gmlp_w_in, gmlp_ln_g, gmlp_ln_b, gmlp_w_s, gmlp_b_s, gmlp_w_out, attn_wq, attn_wk, attn_wv, attn_wo, ffn_w_gate, ffn_w_up, ffn_w_down):
    d = x_prompt.shape[-1]
    dg = gmlp_w_out.shape[1]
    hd = dg // N_SG_HEADS
    bs_full = jnp.repeat(jnp.swapaxes(gmlp_b_s, 1, 2), hd, axis=2)
    p = dict(
        norm_gains=norm_gains,
        pool_w=pool_w.astype(BF16), pool_scale=pool_scale,
        gmlp_wu=gmlp_w_in[:, :, :dg].astype(BF16), gmlp_wv=gmlp_w_in[:, :, dg:].astype(BF16),
        gmlp_ln_g=gmlp_ln_g, gmlp_ln_b=gmlp_ln_b, gmlp_w_s=gmlp_w_s.astype(BF16),
        gmlp_bs_full=bs_full, gmlp_w_out=gmlp_w_out.astype(BF16),
        attn_wq=attn_wq.astype(BF16), attn_wo=attn_wo.astype(BF16),
        ffn_w_gate=ffn_w_gate.astype(BF16), ffn_w_up=ffn_w_up.astype(BF16),
        ffn_w_down=ffn_w_down.astype(BF16),
    )
    wk = attn_wk.astype(BF16)
    wv = attn_wv.astype(BF16)
    gmem = mem_norm[:, None, :]
    outs = []
    for x, mem in ((x_prompt, mem_prompt), (x_sample, mem_sample)):
        kt, v = _kv(mem, gmem, wk, wv)
        outs.append(_trunk(x, kt, v, p))
    return tuple(outs)
```

```python
import functools

import numpy as np
import jax
import jax.numpy as jnp
from jax import lax
from jax.experimental import pallas as pl
from jax.experimental.pallas import tpu as pltpu

F32 = jnp.float32
BF16 = jnp.bfloat16

EPS = 1e-6
POOL_WINDOWS = (2, 4, 8, 16)
POOL_HALO = max(POOL_WINDOWS) // 2
CHUNK = 128
N_SG_HEADS = 8
N_XHEADS = 4

V7X_SUBLANES = 8
V7X_VMEM_LIMIT_BYTES = 56 * 1024 * 1024

GELU_C0 = np.float32(np.sqrt(2.0 / np.pi))
GELU_C1 = np.float32(0.044715)


def _rms(x, g):
    ms = jnp.mean(x * x, axis=-1, keepdims=True)
    return x * lax.rsqrt(ms + EPS) * g


def _gelu_tanh(x):
    return x * (0.5 * (1.0 + jnp.tanh(GELU_C0 * (x + GELU_C1 * (x * x * x)))))


def _dot(a, b):
    return jnp.dot(a, b, preferred_element_type=F32)


def _const_spec(shape):
    return pl.BlockSpec(shape, lambda *_: (0,) * len(shape), pipeline_mode=pl.Buffered(1))


def _layer_spec(shape, layer, col_block=0):
    index = (layer,) + (0,) * (len(shape) - 1) + (col_block,)
    return pl.BlockSpec((1,) + tuple(shape), lambda *_: index, pipeline_mode=pl.Buffered(1))


def _params(semantics):
    return pltpu.CompilerParams(dimension_semantics=semantics,
                                vmem_limit_bytes=V7X_VMEM_LIMIT_BYTES)


def _ffn_kernel(x_ref, gpre_ref, gpost_ref, wg_ref, wu_ref, wd_ref, o_ref, xn_ref, acc_ref):
    k = pl.program_id(1)

    @pl.when(k == 0)
    def _():
        xn_ref[...] = _rms(x_ref[...], gpre_ref[...]).astype(BF16)
        acc_ref[...] = jnp.zeros_like(acc_ref)

    xn = xn_ref[...]
    gate = _dot(xn, wg_ref[0])
    up = _dot(xn, wu_ref[0])
    h = (gate * (1.0 / (1.0 + jnp.exp(-gate))) * up).astype(BF16)
    acc_ref[...] += _dot(h, wd_ref[0])

    @pl.when(k == pl.num_programs(1) - 1)
    def _():
        o_ref[...] = x_ref[...] + _rms(acc_ref[...], gpost_ref[...])


def _ffn(x, gpre, gpost, wg, wu, wd, layer, *, tm, tf):
    t, d = x.shape
    f = wg.shape[2]
    return pl.pallas_call(
        _ffn_kernel,
        grid=(t // tm, f // tf),
        in_specs=[
            pl.BlockSpec((tm, d), lambda i, k: (i, 0)),
            pl.BlockSpec((1, d), lambda i, k: (0, 0)),
            pl.BlockSpec((1, d), lambda i, k: (0, 0)),
            pl.BlockSpec((1, d, tf), lambda i, k: (layer, 0, k)),
            pl.BlockSpec((1, d, tf), lambda i, k: (layer, 0, k)),
            pl.BlockSpec((1, tf, d), lambda i, k: (layer, k, 0)),
        ],
        out_specs=pl.BlockSpec((tm, d), lambda i, k: (i, 0)),
        out_shape=jax.ShapeDtypeStruct((t, d), F32),
        scratch_shapes=[pltpu.VMEM((tm, d), BF16), pltpu.VMEM((tm, d), F32)],
        compiler_params=_params(("parallel", "arbitrary")),
        name="ffn",
    )(x, gpre, gpost, wg, wu, wd)


def _kv_kernel(mem_ref, g_ref, wk_ref, wv_ref, kt_ref, v_ref):
    memn = _rms(mem_ref[0], g_ref[0]).astype(BF16)
    kt_ref[0, 0] = _dot(memn, wk_ref[0]).T.astype(BF16)
    v_ref[0, 0] = _dot(memn, wv_ref[0]).astype(BF16)


def _kv(mem, mem_norm, wk, wv):
    b, m, d = mem.shape
    n_layers = wk.shape[0]
    return pl.pallas_call(
        _kv_kernel,
        grid=(n_layers, b),
        in_specs=[
            pl.BlockSpec((1, m, d), lambda l, i: (i, 0, 0)),
            pl.BlockSpec((1, 1, d), lambda l, i: (l, 0, 0)),
            pl.BlockSpec((1, d, d), lambda l, i: (l, 0, 0)),
            pl.BlockSpec((1, d, d), lambda l, i: (l, 0, 0)),
        ],
        out_specs=[
            pl.BlockSpec((1, 1, d, m), lambda l, i: (l, i, 0, 0)),
            pl.BlockSpec((1, 1, m, d), lambda l, i: (l, i, 0, 0)),
        ],
        out_shape=[
            jax.ShapeDtypeStruct((n_layers, b, d, m), BF16),
            jax.ShapeDtypeStruct((n_layers, b, m, d), BF16),
        ],
        compiler_params=_params(("arbitrary", "parallel")),
        name="memory_kv",
    )(mem, mem_norm, wk, wv)


def _attn_kernel(x_ref, gpre_ref, gpost_ref, wq_ref, kt_ref, v_ref, wo_ref, o_ref):
    x = x_ref[0]
    d = x.shape[-1]
    hd = d // N_XHEADS
    xn = _rms(x, gpre_ref[...]).astype(BF16)
    q = _dot(xn, wq_ref[0]).astype(BF16)
    heads = []
    for h in range(N_XHEADS):
        cols = slice(h * hd, (h + 1) * hd)
        s = _dot(q[:, cols], kt_ref[0, 0, cols, :]) * (hd ** -0.5)
        s = s - jnp.max(s, axis=-1, keepdims=True)
        e = jnp.exp(s)
        p = (e / jnp.sum(e, axis=-1, keepdims=True)).astype(BF16)
        heads.append(_dot(p, v_ref[0, 0, :, cols]).astype(BF16))
    o = jnp.concatenate(heads, axis=-1)
    o_ref[0] = x + _rms(_dot(o, wo_ref[0]), gpost_ref[...])


def _attn(x, gpre, gpost, wq, kt, v, wo, layer, *, tm):
    b, s, d = x.shape
    m = v.shape[2]
    return pl.pallas_call(
        _attn_kernel,
        grid=(b, s // tm),
        in_specs=[
            pl.BlockSpec((1, tm, d), lambda i, j: (i, j, 0)),
            _const_spec((1, d)),
            _const_spec((1, d)),
            _layer_spec((d, d), layer),
            pl.BlockSpec((1, 1, d, m), lambda i, j: (layer, i, 0, 0)),
            pl.BlockSpec((1, 1, m, d), lambda i, j: (layer, i, 0, 0)),
            _layer_spec((d, d), layer),
        ],
        out_specs=pl.BlockSpec((1, tm, d), lambda i, j: (i, j, 0)),
        out_shape=jax.ShapeDtypeStruct((b, s, d), F32),
        compiler_params=_params(("parallel", "parallel")),
        name="cross_attn",
    )(x, gpre, gpost, wq, kt, v, wo)


def _pool_kernel(xprev_ref, x_ref, xnext_ref, gpre_ref, gpost_ref, w_ref, scale_ref, o_ref,
                 ext_ref, *, seq_len):
    j = pl.program_id(1)
    tm, d = x_ref.shape[1], x_ref.shape[2]
    gd = d // len(POOL_WINDOWS)
    gpre = gpre_ref[...]
    x = x_ref[0]
    hprev = _rms(xprev_ref[0], gpre)
    hnext = _rms(xnext_ref[0], gpre)
    ext_ref[0:POOL_HALO, :] = jnp.where(j > 0, hprev, 0.0)
    ext_ref[POOL_HALO:POOL_HALO + tm, :] = _rms(x, gpre)
    ext_ref[POOL_HALO + tm:, :] = jnp.where(j < pl.num_programs(1) - 1, hnext, 0.0)

    t = j * tm + lax.broadcasted_iota(jnp.int32, (tm, 1), 0)
    outs = []
    for g, win in enumerate(POOL_WINDOWS):
        half = win // 2
        cols = slice(g * gd, (g + 1) * gd)
        base = POOL_HALO - half
        wsum = ext_ref[base:base + tm, cols]
        for r in range(1, win):
            wsum = wsum + ext_ref[base + r:base + r + tm, cols]
        count = (jnp.minimum(t + half, seq_len) - jnp.maximum(t - half, 0)).astype(F32)
        diff = wsum / count - ext_ref[POOL_HALO:POOL_HALO + tm, cols]
        outs.append(_dot(diff.astype(BF16), w_ref[0, g]))
    y = jnp.concatenate(outs, axis=-1) * scale_ref[...]
    o_ref[0] = x + _rms(y, gpost_ref[...])


def _pool(x, gpre, gpost, w, scale, layer, *, tm):
    b, s, d = x.shape
    _, n_groups, gd, _ = w.shape
    halo_blocks_per_tile = tm // POOL_HALO
    n_halo_blocks = s // POOL_HALO
    return pl.pallas_call(
        functools.partial(_pool_kernel, seq_len=s),
        grid=(b, s // tm),
        in_specs=[
            pl.BlockSpec((1, POOL_HALO, d),
                         lambda i, j: (i, jnp.maximum(j * halo_blocks_per_tile - 1, 0), 0)),
            pl.BlockSpec((1, tm, d), lambda i, j: (i, j, 0)),
            pl.BlockSpec((1, POOL_HALO, d),
                         lambda i, j: (i, jnp.minimum((j + 1) * halo_blocks_per_tile,
                                                      n_halo_blocks - 1), 0)),
            _const_spec((1, d)),
            _const_spec((1, d)),
            _layer_spec((n_groups, gd, gd), layer),
            _const_spec((1, d)),
        ],
        out_specs=pl.BlockSpec((1, tm, d), lambda i, j: (i, j, 0)),
        out_shape=jax.ShapeDtypeStruct((b, s, d), F32),
        scratch_shapes=[pltpu.VMEM((tm + 2 * POOL_HALO, d), F32)],
        compiler_params=_params(("parallel", "parallel")),
        name="pool_mixer",
    )(x, x, x, gpre, gpost, w, scale)


def _gmlp_kernel(x_ref, gpre_ref, gpost_ref, wu_ref, wv_ref, lng_ref, lnb_ref, ws_ref, bs_ref,
                 wout_ref, o_ref):
    x = x_ref[0]
    tm, d = x.shape
    hd = d // N_SG_HEADS
    xn = _rms(x, gpre_ref[...]).astype(BF16)

    v = _gelu_tanh(_dot(xn, wv_ref[0]))
    mu = jnp.mean(v, axis=-1, keepdims=True)
    vc = v - mu
    var = jnp.mean(vc * vc, axis=-1, keepdims=True)
    vn = (vc * lax.rsqrt(var + EPS) * lng_ref[...] + lnb_ref[...]).astype(BF16)

    rows = []
    for c in range(tm // CHUNK):
        r = slice(c * CHUNK, (c + 1) * CHUNK)
        rows.append(jnp.concatenate(
            [_dot(ws_ref[0, h], vn[r, h * hd:(h + 1) * hd]) for h in range(N_SG_HEADS)], axis=-1)
            + bs_ref[0])
    mixed = jnp.concatenate(rows, axis=0)

    u = _gelu_tanh(_dot(xn, wu_ref[0]))
    gated = (u * mixed).astype(BF16)
    o_ref[0] = x + _rms(_dot(gated, wout_ref[0]), gpost_ref[...])


def _gmlp(x, gpre, gpost, w_in, lng, lnb, ws, bs_full, wout, layer, *, tm):
    b, s, d = x.shape
    dg = wout.shape[1]
    return pl.pallas_call(
        _gmlp_kernel,
        grid=(b, s // tm),
        in_specs=[
            pl.BlockSpec((1, tm, d), lambda i, j: (i, j, 0)),
            _const_spec((1, d)),
            _const_spec((1, d)),
            _layer_spec((d, dg), layer, col_block=0),
            _layer_spec((d, dg), layer, col_block=1),
            _const_spec((1, dg)),
            _const_spec((1, dg)),
            _layer_spec(ws.shape[1:], layer),
            _layer_spec(bs_full.shape[1:], layer),
            _layer_spec((dg, d), layer),
        ],
        out_specs=pl.BlockSpec((1, tm, d), lambda i, j: (i, j, 0)),
        out_shape=jax.ShapeDtypeStruct((b, s, d), F32),
        compiler_params=_params(("parallel", "parallel")),
        name="gmlp_mixer",
    )(x, gpre, gpost, w_in, w_in, lng, lnb, ws, bs_full, wout)


def _trunk(x, kt, v, p):
    b, s, d = x.shape
    depth = p["norm_gains"].shape[0]
    for i in range(depth):
        g = p["norm_gains"][i]
        gain = lambda n: g[n][None, :]
        j = i // 2
        if i % 2 == 0:
            x = _pool(x, gain(0), gain(1), p["pool_w"], p["pool_scale"][j][None, :], j, tm=512)
        else:
            x = _gmlp(x, gain(0), gain(1), p["gmlp_w_in"], p["gmlp_ln_g"][j][None, :],
                      p["gmlp_ln_b"][j][None, :], p["gmlp_w_s"], p["gmlp_bs_full"],
                      p["gmlp_w_out"], j, tm=256)
        x = _attn(x, gain(2), gain(3), p["attn_wq"], kt, v, p["attn_wo"], i, tm=512)
        x = _ffn(x.reshape(b * s, d), gain(4), gain(5), p["ffn_w_gate"], p["ffn_w_up"],
                 p["ffn_w_down"], i, tm=512, tf=512).reshape(b, s, d)
    return x


def kernel(x_prompt, x_sample, mem_prompt, mem_sample, norm_gains, mem_norm, pool_w, pool_scale, gmlp_w_in, gmlp_ln_g, gmlp_ln_b, gmlp_w_s, gmlp_b_s, gmlp_w_out, attn_wq, attn_wk, attn_wv, attn_wo, ffn_w_gate, ffn_w_up, ffn_w_down):
    dg = gmlp_w_out.shape[1]
    hd = dg // N_SG_HEADS
    bs_full = jnp.repeat(jnp.swapaxes(gmlp_b_s, 1, 2), hd, axis=2)
    p = dict(
        norm_gains=norm_gains,
        pool_w=pool_w.astype(BF16), pool_scale=pool_scale,
        gmlp_w_in=gmlp_w_in.astype(BF16),
        gmlp_ln_g=gmlp_ln_g, gmlp_ln_b=gmlp_ln_b, gmlp_w_s=gmlp_w_s.astype(BF16),
        gmlp_bs_full=bs_full, gmlp_w_out=gmlp_w_out.astype(BF16),
        attn_wq=attn_wq.astype(BF16), attn_wo=attn_wo.astype(BF16),
        ffn_w_gate=ffn_w_gate.astype(BF16), ffn_w_up=ffn_w_up.astype(BF16),
        ffn_w_down=ffn_w_down.astype(BF16),
    )
    wk = attn_wk.astype(BF16)
    wv = attn_wv.astype(BF16)
    gmem = mem_norm[:, None, :]
    outs = []
    for x, mem in ((x_prompt, mem_prompt), (x_sample, mem_sample)):
        kt, v = _kv(mem, gmem, wk, wv)
        outs.append(_trunk(x, kt, v, p))
    return tuple(outs)
```

```python
import functools

import numpy as np
import jax
import jax.numpy as jnp
from jax import lax
from jax.experimental import pallas as pl
from jax.experimental.pallas import tpu as pltpu

F32 = jnp.float32
BF16 = jnp.bfloat16

EPS = 1e-6
POOL_WINDOWS = (2, 4, 8, 16)
POOL_HALO = max(POOL_WINDOWS) // 2
CHUNK = 128
N_SG_HEADS = 8
N_XHEADS = 4

V7X_VMEM_LIMIT_BYTES = 60 * 1024 * 1024
ROW_CHUNK = 16
POOL_ROWS = 64

GELU_C0 = np.float32(np.sqrt(2.0 / np.pi))
GELU_C1 = np.float32(0.044715)


def _rms(x, g):
    ms = jnp.mean(x * x, axis=-1, keepdims=True)
    return x * lax.rsqrt(ms + EPS) * g


def _gelu_tanh(x):
    return x * (0.5 * (1.0 + jnp.tanh(GELU_C0 * (x + GELU_C1 * (x * x * x)))))


def _dot(a, b):
    return jnp.dot(a, b, preferred_element_type=F32)


def _row_chunks(n_rows):
    return [slice(r, r + ROW_CHUNK) for r in range(0, n_rows, ROW_CHUNK)]


def _const_spec(shape):
    return pl.BlockSpec(shape, lambda *_: (0,) * len(shape), pipeline_mode=pl.Buffered(1))


def _layer_spec(shape, layer, col_block=0):
    index = (layer,) + (0,) * (len(shape) - 1) + (col_block,)
    return pl.BlockSpec((1,) + tuple(shape), lambda *_: index, pipeline_mode=pl.Buffered(1))


def _params(semantics):
    return pltpu.CompilerParams(dimension_semantics=semantics,
                                vmem_limit_bytes=V7X_VMEM_LIMIT_BYTES)


def _ffn_kernel(x_ref, gpre_ref, gpost_ref, wg_ref, wu_ref, wd_ref, o_ref, xn_ref):
    k = pl.program_id(1)
    tm = x_ref.shape[0]

    @pl.when(k == 0)
    def _():
        gpre = gpre_ref[...]
        for r in _row_chunks(tm):
            xn_ref[r, :] = _rms(x_ref[r, :], gpre).astype(BF16)
        o_ref[...] = jnp.zeros_like(o_ref)

    xn = xn_ref[...]
    gate = _dot(xn, wg_ref[0])
    up = _dot(xn, wu_ref[0])
    h = (gate * (1.0 / (1.0 + jnp.exp(-gate))) * up).astype(BF16)
    o_ref[...] += _dot(h, wd_ref[0])

    @pl.when(k == pl.num_programs(1) - 1)
    def _():
        gpost = gpost_ref[...]
        for r in _row_chunks(tm):
            o_ref[r, :] = x_ref[r, :] + _rms(o_ref[r, :], gpost)


def _ffn(x, gpre, gpost, wg, wu, wd, layer, *, tm, tf):
    t, d = x.shape
    f = wg.shape[2]
    return pl.pallas_call(
        _ffn_kernel,
        grid=(t // tm, f // tf),
        in_specs=[
            pl.BlockSpec((tm, d), lambda i, k: (i, 0)),
            pl.BlockSpec((1, d), lambda i, k: (0, 0)),
            pl.BlockSpec((1, d), lambda i, k: (0, 0)),
            pl.BlockSpec((1, d, tf), lambda i, k: (layer, 0, k)),
            pl.BlockSpec((1, d, tf), lambda i, k: (layer, 0, k)),
            pl.BlockSpec((1, tf, d), lambda i, k: (layer, k, 0)),
        ],
        out_specs=pl.BlockSpec((tm, d), lambda i, k: (i, 0)),
        out_shape=jax.ShapeDtypeStruct((t, d), F32),
        scratch_shapes=[pltpu.VMEM((tm, d), BF16)],
        compiler_params=_params(("parallel", "arbitrary")),
        name="ffn",
    )(x, gpre, gpost, wg, wu, wd)


def _kv_kernel(mem_ref, g_ref, wk_ref, wv_ref, kt_ref, v_ref):
    memn = _rms(mem_ref[0], g_ref[0]).astype(BF16)
    kt_ref[0, 0] = _dot(memn, wk_ref[0]).T.astype(BF16)
    v_ref[0, 0] = _dot(memn, wv_ref[0]).astype(BF16)


def _kv(mem, mem_norm, wk, wv):
    b, m, d = mem.shape
    n_layers = wk.shape[0]
    return pl.pallas_call(
        _kv_kernel,
        grid=(n_layers, b),
        in_specs=[
            pl.BlockSpec((1, m, d), lambda l, i: (i, 0, 0)),
            pl.BlockSpec((1, 1, d), lambda l, i: (l, 0, 0)),
            pl.BlockSpec((1, d, d), lambda l, i: (l, 0, 0)),
            pl.BlockSpec((1, d, d), lambda l, i: (l, 0, 0)),
        ],
        out_specs=[
            pl.BlockSpec((1, 1, d, m), lambda l, i: (l, i, 0, 0)),
            pl.BlockSpec((1, 1, m, d), lambda l, i: (l, i, 0, 0)),
        ],
        out_shape=[
            jax.ShapeDtypeStruct((n_layers, b, d, m), BF16),
            jax.ShapeDtypeStruct((n_layers, b, m, d), BF16),
        ],
        compiler_params=_params(("arbitrary", "parallel")),
        name="memory_kv",
    )(mem, mem_norm, wk, wv)


def _attn_kernel(x_ref, gpre_ref, gpost_ref, wq_ref, kt_ref, v_ref, wo_ref, o_ref, xn_ref):
    tm, d = x_ref.shape[1], x_ref.shape[2]
    hd = d // N_XHEADS
    gpre = gpre_ref[...]
    for r in _row_chunks(tm):
        xn_ref[r, :] = _rms(x_ref[0, r, :], gpre).astype(BF16)
    q = _dot(xn_ref[...], wq_ref[0]).astype(BF16)
    heads = []
    for h in range(N_XHEADS):
        cols = slice(h * hd, (h + 1) * hd)
        s = _dot(q[:, cols], kt_ref[0, 0, cols, :]) * (hd ** -0.5)
        s = s - jnp.max(s, axis=-1, keepdims=True)
        e = jnp.exp(s)
        p = (e / jnp.sum(e, axis=-1, keepdims=True)).astype(BF16)
        heads.append(_dot(p, v_ref[0, 0, :, cols]).astype(BF16))
    o = jnp.concatenate(heads, axis=-1)
    o_ref[0] = _dot(o, wo_ref[0])
    gpost = gpost_ref[...]
    for r in _row_chunks(tm):
        o_ref[0, r, :] = x_ref[0, r, :] + _rms(o_ref[0, r, :], gpost)


def _attn(x, gpre, gpost, wq, kt, v, wo, layer, *, tm):
    b, s, d = x.shape
    m = v.shape[2]
    return pl.pallas_call(
        _attn_kernel,
        grid=(b, s // tm),
        in_specs=[
            pl.BlockSpec((1, tm, d), lambda i, j: (i, j, 0)),
            _const_spec((1, d)),
            _const_spec((1, d)),
            _layer_spec((d, d), layer),
            pl.BlockSpec((1, 1, d, m), lambda i, j: (layer, i, 0, 0)),
            pl.BlockSpec((1, 1, m, d), lambda i, j: (layer, i, 0, 0)),
            _layer_spec((d, d), layer),
        ],
        out_specs=pl.BlockSpec((1, tm, d), lambda i, j: (i, j, 0)),
        out_shape=jax.ShapeDtypeStruct((b, s, d), F32),
        scratch_shapes=[pltpu.VMEM((tm, d), BF16)],
        compiler_params=_params(("parallel", "parallel")),
        name="cross_attn",
    )(x, gpre, gpost, wq, kt, v, wo)


def _pool_kernel(xprev_ref, x_ref, xnext_ref, gpre_ref, gpost_ref, w_ref, scale_ref, o_ref,
                 ext_ref, diff_ref, *, seq_len):
    j = pl.program_id(1)
    tm, d = x_ref.shape[1], x_ref.shape[2]
    gd = d // len(POOL_WINDOWS)
    gpre = gpre_ref[...]
    hprev = _rms(xprev_ref[0], gpre)
    hnext = _rms(xnext_ref[0], gpre)
    ext_ref[0:POOL_HALO, :] = jnp.where(j > 0, hprev, 0.0)
    for r in _row_chunks(tm):
        ext_ref[POOL_HALO + r.start:POOL_HALO + r.stop, :] = _rms(x_ref[0, r, :], gpre)
    ext_ref[POOL_HALO + tm:, :] = jnp.where(j < pl.num_programs(1) - 1, hnext, 0.0)

    n_ext = POOL_ROWS + 2 * POOL_HALO

    def shift_up(a, k):
        return pltpu.roll(a, n_ext - k, axis=0)

    for r0 in range(0, tm, POOL_ROWS):
        t = j * tm + r0 + lax.broadcasted_iota(jnp.int32, (POOL_ROWS, 1), 0)
        for g, win in enumerate(POOL_WINDOWS):
            half = win // 2
            cols = slice(g * gd, (g + 1) * gd)
            run = ext_ref[r0:r0 + n_ext, cols]
            width = 1
            while width < half:
                run = run + shift_up(run, width)
                width *= 2
            lo = run if half == POOL_HALO else shift_up(run, POOL_HALO - half)
            wsum = lo[:POOL_ROWS] + run[POOL_HALO:POOL_HALO + POOL_ROWS]
            count = (jnp.minimum(t + half, seq_len) - jnp.maximum(t - half, 0)).astype(F32)
            centre = ext_ref[POOL_HALO + r0:POOL_HALO + r0 + POOL_ROWS, cols]
            diff_ref[r0:r0 + POOL_ROWS, cols] = (wsum * (1.0 / count) - centre).astype(BF16)

    for g in range(len(POOL_WINDOWS)):
        cols = slice(g * gd, (g + 1) * gd)
        o_ref[0, :, cols] = _dot(diff_ref[:, cols], w_ref[0, g]) * scale_ref[:, cols]
    gpost = gpost_ref[...]
    for r in _row_chunks(tm):
        o_ref[0, r, :] = x_ref[0, r, :] + _rms(o_ref[0, r, :], gpost)


def _pool(x, gpre, gpost, w, scale, layer, *, tm):
    b, s, d = x.shape
    _, n_groups, gd, _ = w.shape
    halo_blocks_per_tile = tm // POOL_HALO
    n_halo_blocks = s // POOL_HALO
    return pl.pallas_call(
        functools.partial(_pool_kernel, seq_len=s),
        grid=(b, s // tm),
        in_specs=[
            pl.BlockSpec((1, POOL_HALO, d),
                         lambda i, j: (i, jnp.maximum(j * halo_blocks_per_tile - 1, 0), 0)),
            pl.BlockSpec((1, tm, d), lambda i, j: (i, j, 0)),
            pl.BlockSpec((1, POOL_HALO, d),
                         lambda i, j: (i, jnp.minimum((j + 1) * halo_blocks_per_tile,
                                                      n_halo_blocks - 1), 0)),
            _const_spec((1, d)),
            _const_spec((1, d)),
            _layer_spec((n_groups, gd, gd), layer),
            _const_spec((1, d)),
        ],
        out_specs=pl.BlockSpec((1, tm, d), lambda i, j: (i, j, 0)),
        out_shape=jax.ShapeDtypeStruct((b, s, d), F32),
        scratch_shapes=[pltpu.VMEM((tm + 2 * POOL_HALO, d), F32), pltpu.VMEM((tm, d), BF16)],
        compiler_params=_params(("parallel", "parallel")),
        name="pool_mixer",
    )(x, x, x, gpre, gpost, w, scale)


def _gmlp_kernel(x_ref, gpre_ref, gpost_ref, wu_ref, wv_ref, lng_ref, lnb_ref, ws_ref, bs_ref,
                 wout_ref, o_ref, xn_ref, u_ref, v_ref, vn_ref, mix_ref, gated_ref):
    tm, d = x_ref.shape[1], x_ref.shape[2]
    dg = u_ref.shape[1]
    hd = dg // N_SG_HEADS
    gpre = gpre_ref[...]
    for r in _row_chunks(tm):
        xn_ref[r, :] = _rms(x_ref[0, r, :], gpre).astype(BF16)
    xn = xn_ref[...]

    v_ref[...] = _dot(xn, wv_ref[0])
    u_ref[...] = _dot(xn, wu_ref[0])
    lng, lnb = lng_ref[...], lnb_ref[...]
    for r in _row_chunks(tm):
        v = _gelu_tanh(v_ref[r, :])
        vc = v - jnp.mean(v, axis=-1, keepdims=True)
        var = jnp.mean(vc * vc, axis=-1, keepdims=True)
        vn_ref[r, :] = (vc * lax.rsqrt(var + EPS) * lng + lnb).astype(BF16)

    for c in range(tm // CHUNK):
        rows = slice(c * CHUNK, (c + 1) * CHUNK)
        for h in range(N_SG_HEADS):
            cols = slice(h * hd, (h + 1) * hd)
            mix_ref[rows, cols] = _dot(ws_ref[0, h], vn_ref[rows, cols])

    for r in _row_chunks(tm):
        bias = bs_ref[0, r.start % CHUNK:r.start % CHUNK + ROW_CHUNK, :]
        gated_ref[r, :] = (_gelu_tanh(u_ref[r, :]) * (mix_ref[r, :] + bias)).astype(BF16)

    o_ref[0] = _dot(gated_ref[...], wout_ref[0])
    gpost = gpost_ref[...]
    for r in _row_chunks(tm):
        o_ref[0, r, :] = x_ref[0, r, :] + _rms(o_ref[0, r, :], gpost)


def _gmlp(x, gpre, gpost, w_in, lng, lnb, ws, bs_full, wout, layer, *, tm):
    b, s, d = x.shape
    dg = wout.shape[1]
    return pl.pallas_call(
        _gmlp_kernel,
        grid=(b, s // tm),
        in_specs=[
            pl.BlockSpec((1, tm, d), lambda i, j: (i, j, 0)),
            _const_spec((1, d)),
            _const_spec((1, d)),
            _layer_spec((d, dg), layer, col_block=0),
            _layer_spec((d, dg), layer, col_block=1),
            _const_spec((1, dg)),
            _const_spec((1, dg)),
            _layer_spec(ws.shape[1:], layer),
            _layer_spec(bs_full.shape[1:], layer),
            _layer_spec((dg, d), layer),
        ],
        out_specs=pl.BlockSpec((1, tm, d), lambda i, j: (i, j, 0)),
        out_shape=jax.ShapeDtypeStruct((b, s, d), F32),
        scratch_shapes=[
            pltpu.VMEM((tm, d), BF16),
            pltpu.VMEM((tm, dg), F32),
            pltpu.VMEM((tm, dg), F32),
            pltpu.VMEM((tm, dg), BF16),
            pltpu.VMEM((tm, dg), F32),
            pltpu.VMEM((tm, dg), BF16),
        ],
        compiler_params=_params(("parallel", "parallel")),
        name="gmlp_mixer",
    )(x, gpre, gpost, w_in, w_in, lng, lnb, ws, bs_full, wout)


def _trunk(x, kt, v, p):
    b, s, d = x.shape
    depth = p["norm_gains"].shape[0]
    for i in range(depth):
        g = p["norm_gains"][i]
        gain = lambda n: g[n][None, :]
        j = i // 2
        if i % 2 == 0:
            x = _pool(x, gain(0), gain(1), p["pool_w"], p["pool_scale"][j][None, :], j, tm=512)
        else:
            x = _gmlp(x, gain(0), gain(1), p["gmlp_w_in"], p["gmlp_ln_g"][j][None, :],
                      p["gmlp_ln_b"][j][None, :], p["gmlp_w_s"], p["gmlp_bs_full"],
                      p["gmlp_w_out"], j, tm=256)
        x = _attn(x, gain(2), gain(3), p["attn_wq"], kt, v, p["attn_wo"], i, tm=512)
        x = _ffn(x.reshape(b * s, d), gain(4), gain(5), p["ffn_w_gate"], p["ffn_w_up"],
                 p["ffn_w_down"], i, tm=1024, tf=512).reshape(b, s, d)
    return x


def kernel(x_prompt, x_sample, mem_prompt, mem_sample, norm_gains, mem_norm, pool_w, pool_scale, gmlp_w_in, gmlp_ln_g, gmlp_ln_b, gmlp_w_s, gmlp_b_s, gmlp_w_out, attn_wq, attn_wk, attn_wv, attn_wo, ffn_w_gate, ffn_w_up, ffn_w_down):
    dg = gmlp_w_out.shape[1]
    hd = dg // N_SG_HEADS
    bs_full = jnp.repeat(jnp.swapaxes(gmlp_b_s, 1, 2), hd, axis=2)
    p = dict(
        norm_gains=norm_gains,
        pool_w=pool_w.astype(BF16), pool_scale=pool_scale,
        gmlp_w_in=gmlp_w_in.astype(BF16),
        gmlp_ln_g=gmlp_ln_g, gmlp_ln_b=gmlp_ln_b, gmlp_w_s=gmlp_w_s.astype(BF16),
        gmlp_bs_full=bs_full, gmlp_w_out=gmlp_w_out.astype(BF16),
        attn_wq=attn_wq.astype(BF16), attn_wo=attn_wo.astype(BF16),
        ffn_w_gate=ffn_w_gate.astype(BF16), ffn_w_up=ffn_w_up.astype(BF16),
        ffn_w_down=ffn_w_down.astype(BF16),
    )
    wk = attn_wk.astype(BF16)
    wv = attn_wv.astype(BF16)
    gmem = mem_norm[:, None, :]
    outs = []
    for x, mem in ((x_prompt, mem_prompt), (x_sample, mem_sample)):
        kt, v = _kv(mem, gmem, wk, wv)
        outs.append(_trunk(x, kt, v, p))
    return tuple(outs)
```

```python
import functools

import numpy as np
import jax
import jax.numpy as jnp
from jax import lax
from jax.experimental import pallas as pl
from jax.experimental.pallas import tpu as pltpu

F32 = jnp.float32
BF16 = jnp.bfloat16

EPS = 1e-6
POOL_WINDOWS = (2, 4, 8, 16)
POOL_HALO = max(POOL_WINDOWS) // 2
CHUNK = 128
N_SG_HEADS = 8
N_XHEADS = 4

V7X_VMEM_LIMIT_BYTES = 60 * 1024 * 1024
ROW_CHUNK = 16
POOL_ROWS = 64

GELU_C0 = np.float32(np.sqrt(2.0 / np.pi))
GELU_C1 = np.float32(0.044715)


def _rms(x, g):
    ms = jnp.mean(x * x, axis=-1, keepdims=True)
    return x * lax.rsqrt(ms + EPS) * g


def _gelu_tanh(x):
    return x * (0.5 * (1.0 + jnp.tanh(GELU_C0 * (x + GELU_C1 * (x * x * x)))))


def _dot(a, b):
    return jnp.dot(a, b, preferred_element_type=F32)


def _row_chunks(n_rows):
    return [slice(r, r + ROW_CHUNK) for r in range(0, n_rows, ROW_CHUNK)]


def _const_spec(shape):
    return pl.BlockSpec(shape, lambda *_: (0,) * len(shape), pipeline_mode=pl.Buffered(1))


def _layer_spec(shape, layer, col_block=0):
    index = (layer,) + (0,) * (len(shape) - 1) + (col_block,)
    return pl.BlockSpec((1,) + tuple(shape), lambda *_: index, pipeline_mode=pl.Buffered(1))


def _params(semantics):
    return pltpu.CompilerParams(dimension_semantics=semantics,
                                vmem_limit_bytes=V7X_VMEM_LIMIT_BYTES)


def _ffn_kernel(x_ref, gpre_ref, gpost_ref, wg_ref, wu_ref, wd_ref, o_ref, xn_ref):
    k = pl.program_id(1)
    tm = x_ref.shape[0]

    @pl.when(k == 0)
    def _():
        gpre = gpre_ref[...]
        for r in _row_chunks(tm):
            xn_ref[r, :] = _rms(x_ref[r, :], gpre).astype(BF16)
        o_ref[...] = jnp.zeros_like(o_ref)

    xn = xn_ref[...]
    gate = _dot(xn, wg_ref[0])
    up = _dot(xn, wu_ref[0])
    h = (gate * (1.0 / (1.0 + jnp.exp(-gate))) * up).astype(BF16)
    o_ref[...] += _dot(h, wd_ref[0])

    @pl.when(k == pl.num_programs(1) - 1)
    def _():
        gpost = gpost_ref[...]
        for r in _row_chunks(tm):
            o_ref[r, :] = x_ref[r, :] + _rms(o_ref[r, :], gpost)


def _ffn(x, gpre, gpost, wg, wu, wd, layer, *, tm, tf, row_start=0, n_rows=None):
    t, d = x.shape
    t = t if n_rows is None else n_rows
    first_tile = row_start // tm
    f = wg.shape[2]
    return pl.pallas_call(
        _ffn_kernel,
        grid=(t // tm, f // tf),
        in_specs=[
            pl.BlockSpec((tm, d), lambda i, k: (i + first_tile, 0)),
            pl.BlockSpec((1, d), lambda i, k: (0, 0)),
            pl.BlockSpec((1, d), lambda i, k: (0, 0)),
            pl.BlockSpec((1, d, tf), lambda i, k: (layer, 0, k)),
            pl.BlockSpec((1, d, tf), lambda i, k: (layer, 0, k)),
            pl.BlockSpec((1, tf, d), lambda i, k: (layer, k, 0)),
        ],
        out_specs=pl.BlockSpec((tm, d), lambda i, k: (i, 0)),
        out_shape=jax.ShapeDtypeStruct((t, d), F32),
        scratch_shapes=[pltpu.VMEM((tm, d), BF16)],
        compiler_params=_params(("parallel", "arbitrary")),
        name="ffn",
    )(x, gpre, gpost, wg, wu, wd)


def _kv_kernel(mem_ref, g_ref, wk_ref, wv_ref, kt_ref, v_ref):
    memn = _rms(mem_ref[0], g_ref[0]).astype(BF16)
    kt_ref[0, 0] = _dot(memn, wk_ref[0]).T.astype(BF16)
    v_ref[0, 0] = _dot(memn, wv_ref[0]).astype(BF16)


def _kv(mem, mem_norm, wk, wv):
    b, m, d = mem.shape
    n_layers = wk.shape[0]
    return pl.pallas_call(
        _kv_kernel,
        grid=(n_layers, b),
        in_specs=[
            pl.BlockSpec((1, m, d), lambda l, i: (i, 0, 0)),
            pl.BlockSpec((1, 1, d), lambda l, i: (l, 0, 0)),
            pl.BlockSpec((1, d, d), lambda l, i: (l, 0, 0)),
            pl.BlockSpec((1, d, d), lambda l, i: (l, 0, 0)),
        ],
        out_specs=[
            pl.BlockSpec((1, 1, d, m), lambda l, i: (l, i, 0, 0)),
            pl.BlockSpec((1, 1, m, d), lambda l, i: (l, i, 0, 0)),
        ],
        out_shape=[
            jax.ShapeDtypeStruct((n_layers, b, d, m), BF16),
            jax.ShapeDtypeStruct((n_layers, b, m, d), BF16),
        ],
        compiler_params=_params(("arbitrary", "parallel")),
        name="memory_kv",
    )(mem, mem_norm, wk, wv)


def _attn_kernel(x_ref, gpre_ref, gpost_ref, wq_ref, kt_ref, v_ref, wo_ref, o_ref, xn_ref):
    tm, d = x_ref.shape[1], x_ref.shape[2]
    hd = d // N_XHEADS
    gpre = gpre_ref[...]
    for r in _row_chunks(tm):
        xn_ref[r, :] = _rms(x_ref[0, r, :], gpre).astype(BF16)
    q = _dot(xn_ref[...], wq_ref[0]).astype(BF16)
    heads = []
    for h in range(N_XHEADS):
        cols = slice(h * hd, (h + 1) * hd)
        s = _dot(q[:, cols], kt_ref[0, 0, cols, :]) * (hd ** -0.5)
        s = s - jnp.max(s, axis=-1, keepdims=True)
        e = jnp.exp(s)
        p = (e / jnp.sum(e, axis=-1, keepdims=True)).astype(BF16)
        heads.append(_dot(p, v_ref[0, 0, :, cols]).astype(BF16))
    o = jnp.concatenate(heads, axis=-1)
    o_ref[0] = _dot(o, wo_ref[0])
    gpost = gpost_ref[...]
    for r in _row_chunks(tm):
        o_ref[0, r, :] = x_ref[0, r, :] + _rms(o_ref[0, r, :], gpost)


def _attn(x, gpre, gpost, wq, kt, v, wo, layer, *, tm):
    b, s, d = x.shape
    m = v.shape[2]
    return pl.pallas_call(
        _attn_kernel,
        grid=(b, s // tm),
        in_specs=[
            pl.BlockSpec((1, tm, d), lambda i, j: (i, j, 0)),
            _const_spec((1, d)),
            _const_spec((1, d)),
            _layer_spec((d, d), layer),
            pl.BlockSpec((1, 1, d, m), lambda i, j: (layer, i, 0, 0)),
            pl.BlockSpec((1, 1, m, d), lambda i, j: (layer, i, 0, 0)),
            _layer_spec((d, d), layer),
        ],
        out_specs=pl.BlockSpec((1, tm, d), lambda i, j: (i, j, 0)),
        out_shape=jax.ShapeDtypeStruct((b, s, d), F32),
        scratch_shapes=[pltpu.VMEM((tm, d), BF16)],
        compiler_params=_params(("parallel", "parallel")),
        name="cross_attn",
    )(x, gpre, gpost, wq, kt, v, wo)


def _pool_kernel(*refs, seq_len, split):
    n_src = 1 if split is None else 2
    sources = [refs[3 * n:3 * n + 3] for n in range(n_src)]
    gpre_ref, gpost_ref, w_ref, scale_ref, o_ref, ext_ref, diff_ref = refs[3 * n_src:]
    i = pl.program_id(0)
    j = pl.program_id(1)
    tm, d = o_ref.shape[1], o_ref.shape[2]
    gd = d // len(POOL_WINDOWS)

    def for_active_source(fn):
        if split is None:
            fn(*sources[0])
        else:
            pl.when(i < split)(lambda: fn(*sources[0]))
            pl.when(i >= split)(lambda: fn(*sources[1]))

    def fill_ext(xprev_ref, x_ref, xnext_ref):
        gpre = gpre_ref[...]
        hprev = _rms(xprev_ref[0], gpre)
        hnext = _rms(xnext_ref[0], gpre)
        ext_ref[0:POOL_HALO, :] = jnp.where(j > 0, hprev, 0.0)
        for r in _row_chunks(tm):
            ext_ref[POOL_HALO + r.start:POOL_HALO + r.stop, :] = _rms(x_ref[0, r, :], gpre)
        ext_ref[POOL_HALO + tm:, :] = jnp.where(j < pl.num_programs(1) - 1, hnext, 0.0)

    for_active_source(fill_ext)

    n_ext = POOL_ROWS + 2 * POOL_HALO

    def shift_up(a, k):
        return pltpu.roll(a, n_ext - k, axis=0)

    for r0 in range(0, tm, POOL_ROWS):
        t = j * tm + r0 + lax.broadcasted_iota(jnp.int32, (POOL_ROWS, 1), 0)
        for g, win in enumerate(POOL_WINDOWS):
            half = win // 2
            cols = slice(g * gd, (g + 1) * gd)
            run = ext_ref[r0:r0 + n_ext, cols]
            width = 1
            while width < half:
                run = run + shift_up(run, width)
                width *= 2
            lo = run if half == POOL_HALO else shift_up(run, POOL_HALO - half)
            wsum = lo[:POOL_ROWS] + run[POOL_HALO:POOL_HALO + POOL_ROWS]
            count = (jnp.minimum(t + half, seq_len) - jnp.maximum(t - half, 0)).astype(F32)
            centre = ext_ref[POOL_HALO + r0:POOL_HALO + r0 + POOL_ROWS, cols]
            diff_ref[r0:r0 + POOL_ROWS, cols] = (wsum * (1.0 / count) - centre).astype(BF16)

    for g in range(len(POOL_WINDOWS)):
        cols = slice(g * gd, (g + 1) * gd)
        o_ref[0, :, cols] = _dot(diff_ref[:, cols], w_ref[0, g]) * scale_ref[:, cols]

    def add_residual(xprev_ref, x_ref, xnext_ref):
        gpost = gpost_ref[...]
        for r in _row_chunks(tm):
            o_ref[0, r, :] = x_ref[0, r, :] + _rms(o_ref[0, r, :], gpost)

    for_active_source(add_residual)


def _pool(xs, gpre, gpost, w, scale, layer, *, tm):
    s, d = xs[0].shape[1:]
    _, n_groups, gd, _ = w.shape
    n_tiles = s // tm
    halo_blocks_per_tile = tm // POOL_HALO
    n_halo_blocks = s // POOL_HALO
    split = xs[0].shape[0] if len(xs) == 2 else None
    b = sum(x.shape[0] for x in xs)

    def source_specs(n):
        def locate(i, j):
            if split is None:
                return i, j
            if n == 0:
                return jnp.minimum(i, split - 1), jnp.where(i < split, j, n_tiles - 1)
            return jnp.maximum(i - split, 0), jnp.where(i >= split, j, 0)

        def prev_map(i, j):
            bi, tj = locate(i, j)
            return bi, jnp.maximum(tj * halo_blocks_per_tile - 1, 0), 0

        def tile_map(i, j):
            bi, tj = locate(i, j)
            return bi, tj, 0

        def next_map(i, j):
            bi, tj = locate(i, j)
            return bi, jnp.minimum((tj + 1) * halo_blocks_per_tile, n_halo_blocks - 1), 0

        return [pl.BlockSpec((1, POOL_HALO, d), prev_map), pl.BlockSpec((1, tm, d), tile_map),
                pl.BlockSpec((1, POOL_HALO, d), next_map)]

    return pl.pallas_call(
        functools.partial(_pool_kernel, seq_len=s, split=split),
        grid=(b, n_tiles),
        in_specs=[spec for n in range(len(xs)) for spec in source_specs(n)] + [
            _const_spec((1, d)),
            _const_spec((1, d)),
            _layer_spec((n_groups, gd, gd), layer),
            _const_spec((1, d)),
        ],
        out_specs=pl.BlockSpec((1, tm, d), lambda i, j: (i, j, 0)),
        out_shape=jax.ShapeDtypeStruct((b, s, d), F32),
        scratch_shapes=[pltpu.VMEM((tm + 2 * POOL_HALO, d), F32), pltpu.VMEM((tm, d), BF16)],
        compiler_params=_params(("parallel", "parallel")),
        name="pool_mixer",
    )(*[x for x in xs for _ in range(3)], gpre, gpost, w, scale)


def _gmlp_kernel(x_ref, gpre_ref, gpost_ref, wu_ref, wv_ref, lng_ref, lnb_ref, ws_ref, bs_ref,
                 wout_ref, o_ref, xn_ref, u_ref, v_ref, vn_ref, mix_ref, gated_ref):
    tm, d = x_ref.shape[1], x_ref.shape[2]
    dg = u_ref.shape[1]
    hd = dg // N_SG_HEADS
    gpre = gpre_ref[...]
    for r in _row_chunks(tm):
        xn_ref[r, :] = _rms(x_ref[0, r, :], gpre).astype(BF16)
    xn = xn_ref[...]

    v_ref[...] = _dot(xn, wv_ref[0])
    u_ref[...] = _dot(xn, wu_ref[0])
    lng, lnb = lng_ref[...], lnb_ref[...]
    for r in _row_chunks(tm):
        v = _gelu_tanh(v_ref[r, :])
        vc = v - jnp.mean(v, axis=-1, keepdims=True)
        var = jnp.mean(vc * vc, axis=-1, keepdims=True)
        vn_ref[r, :] = (vc * lax.rsqrt(var + EPS) * lng + lnb).astype(BF16)

    for c in range(tm // CHUNK):
        rows = slice(c * CHUNK, (c + 1) * CHUNK)
        for h in range(N_SG_HEADS):
            cols = slice(h * hd, (h + 1) * hd)
            mix_ref[rows, cols] = _dot(ws_ref[0, h], vn_ref[rows, cols])

    for r in _row_chunks(tm):
        bias = bs_ref[0, r.start % CHUNK:r.start % CHUNK + ROW_CHUNK, :]
        gated_ref[r, :] = (_gelu_tanh(u_ref[r, :]) * (mix_ref[r, :] + bias)).astype(BF16)

    o_ref[0] = _dot(gated_ref[...], wout_ref[0])
    gpost = gpost_ref[...]
    for r in _row_chunks(tm):
        o_ref[0, r, :] = x_ref[0, r, :] + _rms(o_ref[0, r, :], gpost)


def _gmlp(x, gpre, gpost, w_in, lng, lnb, ws, bs_full, wout, layer, *, tm):
    b, s, d = x.shape
    dg = wout.shape[1]
    return pl.pallas_call(
        _gmlp_kernel,
        grid=(b, s // tm),
        in_specs=[
            pl.BlockSpec((1, tm, d), lambda i, j: (i, j, 0)),
            _const_spec((1, d)),
            _const_spec((1, d)),
            _layer_spec((d, dg), layer, col_block=0),
            _layer_spec((d, dg), layer, col_block=1),
            _const_spec((1, dg)),
            _const_spec((1, dg)),
            _layer_spec(ws.shape[1:], layer),
            _layer_spec(bs_full.shape[1:], layer),
            _layer_spec((dg, d), layer),
        ],
        out_specs=pl.BlockSpec((1, tm, d), lambda i, j: (i, j, 0)),
        out_shape=jax.ShapeDtypeStruct((b, s, d), F32),
        scratch_shapes=[
            pltpu.VMEM((tm, d), BF16),
            pltpu.VMEM((tm, dg), F32),
            pltpu.VMEM((tm, dg), F32),
            pltpu.VMEM((tm, dg), BF16),
            pltpu.VMEM((tm, dg), F32),
            pltpu.VMEM((tm, dg), BF16),
        ],
        compiler_params=_params(("parallel", "parallel")),
        name="gmlp_mixer",
    )(x, gpre, gpost, w_in, w_in, lng, lnb, ws, bs_full, wout)


def _trunk(xs, kt, v, p):
    s, d = xs[0].shape[1:]
    group_rows = [x.shape[0] * s for x in xs]
    b = sum(x.shape[0] for x in xs)
    depth = p["norm_gains"].shape[0]
    ffn_tiles = dict(tm=1024, tf=512)
    for i in range(depth):
        g = p["norm_gains"][i]
        gain = lambda n: g[n][None, :]
        j = i // 2
        if i % 2 == 0:
            x = _pool(xs if i == 0 else (x,), gain(0), gain(1), p["pool_w"],
                      p["pool_scale"][j][None, :], j, tm=512)
        else:
            x = _gmlp(x, gain(0), gain(1), p["gmlp_w_in"], p["gmlp_ln_g"][j][None, :],
                      p["gmlp_ln_b"][j][None, :], p["gmlp_w_s"], p["gmlp_bs_full"],
                      p["gmlp_w_out"], j, tm=256)
        x = _attn(x, gain(2), gain(3), p["attn_wq"], kt, v, p["attn_wo"], i, tm=512)
        ffn_args = (x.reshape(b * s, d), gain(4), gain(5), p["ffn_w_gate"], p["ffn_w_up"],
                    p["ffn_w_down"], i)
        if i < depth - 1:
            x = _ffn(*ffn_args, **ffn_tiles).reshape(b, s, d)
    starts = np.cumsum([0] + group_rows[:-1])
    return tuple(_ffn(*ffn_args, **ffn_tiles, row_start=int(r0), n_rows=n).reshape(x_in.shape)
                 for x_in, r0, n in zip(xs, starts, group_rows))


def kernel(x_prompt, x_sample, mem_prompt, mem_sample, norm_gains, mem_norm, pool_w, pool_scale, gmlp_w_in, gmlp_ln_g, gmlp_ln_b, gmlp_w_s, gmlp_b_s, gmlp_w_out, attn_wq, attn_wk, attn_wv, attn_wo, ffn_w_gate, ffn_w_up, ffn_w_down):
    dg = gmlp_w_out.shape[1]
    hd = dg // N_SG_HEADS
    bs_full = jnp.repeat(jnp.swapaxes(gmlp_b_s, 1, 2), hd, axis=2)
    p = dict(
        norm_gains=norm_gains,
        pool_w=pool_w.astype(BF16), pool_scale=pool_scale,
        gmlp_w_in=gmlp_w_in.astype(BF16),
        gmlp_ln_g=gmlp_ln_g, gmlp_ln_b=gmlp_ln_b, gmlp_w_s=gmlp_w_s.astype(BF16),
        gmlp_bs_full=bs_full, gmlp_w_out=gmlp_w_out.astype(BF16),
        attn_wq=attn_wq.astype(BF16), attn_wo=attn_wo.astype(BF16),
        ffn_w_gate=ffn_w_gate.astype(BF16), ffn_w_up=ffn_w_up.astype(BF16),
        ffn_w_down=ffn_w_down.astype(BF16),
    )
    wk = attn_wk.astype(BF16)
    wv = attn_wv.astype(BF16)
    kt, v = _kv(jnp.concatenate([mem_prompt, mem_sample], axis=0), mem_norm[:, None, :], wk, wv)
    return _trunk((x_prompt, x_sample), kt, v, p)
```

```python
import functools

import numpy as np
import jax
import jax.numpy as jnp
from jax import lax
from jax.experimental import pallas as pl
from jax.experimental.pallas import tpu as pltpu

F32 = jnp.float32
BF16 = jnp.bfloat16

EPS = 1e-6
POOL_WINDOWS = (2, 4, 8, 16)
POOL_HALO = max(POOL_WINDOWS) // 2
CHUNK = 128
N_SG_HEADS = 8
N_XHEADS = 4

V7X_VMEM_LIMIT_BYTES = 60 * 1024 * 1024
ROW_CHUNK = 16
POOL_ROWS = 64
CAST_STEPS = 32

GELU_C0 = np.float32(np.sqrt(2.0 / np.pi))
GELU_C1 = np.float32(0.044715)


def _rms(x, g):
    ms = jnp.mean(x * x, axis=-1, keepdims=True)
    return x * lax.rsqrt(ms + EPS) * g


def _gelu_tanh(x):
    return x * (0.5 * (1.0 + jnp.tanh(GELU_C0 * (x + GELU_C1 * (x * x * x)))))


def _dot(a, b):
    return jnp.dot(a, b, preferred_element_type=F32)


def _row_chunks(n_rows):
    return [slice(r, r + ROW_CHUNK) for r in range(0, n_rows, ROW_CHUNK)]


def _const_spec(shape):
    return pl.BlockSpec(shape, lambda *_: (0,) * len(shape), pipeline_mode=pl.Buffered(1))


def _layer_spec(shape, layer, col_block=0):
    index = (layer,) + (0,) * (len(shape) - 1) + (col_block,)
    return pl.BlockSpec((1,) + tuple(shape), lambda *_: index, pipeline_mode=pl.Buffered(1))


def _params(semantics):
    return pltpu.CompilerParams(dimension_semantics=semantics,
                                vmem_limit_bytes=V7X_VMEM_LIMIT_BYTES)


def _ffn_kernel(x_ref, gpre_ref, gpost_ref, wg_ref, wu_ref, wd_ref, o_ref, xn_ref):
    k = pl.program_id(1)
    tm = x_ref.shape[0]

    @pl.when(k == 0)
    def _():
        gpre = gpre_ref[...]
        for r in _row_chunks(tm):
            xn_ref[r, :] = _rms(x_ref[r, :], gpre).astype(BF16)
        o_ref[...] = jnp.zeros_like(o_ref)

    xn = xn_ref[...]
    gate = _dot(xn, wg_ref[0])
    up = _dot(xn, wu_ref[0])
    h = (gate * (1.0 / (1.0 + jnp.exp(-gate))) * up).astype(BF16)
    o_ref[...] += _dot(h, wd_ref[0])

    @pl.when(k == pl.num_programs(1) - 1)
    def _():
        gpost = gpost_ref[...]
        for r in _row_chunks(tm):
            o_ref[r, :] = x_ref[r, :] + _rms(o_ref[r, :], gpost)


def _ffn(x, gpre, gpost, wg, wu, wd, layer, *, tm, tf, row_start=0, n_rows=None):
    t, d = x.shape
    t = t if n_rows is None else n_rows
    first_tile = row_start // tm
    f = wg.shape[2]
    return pl.pallas_call(
        _ffn_kernel,
        grid=(t // tm, f // tf),
        in_specs=[
            pl.BlockSpec((tm, d), lambda i, k: (i + first_tile, 0)),
            pl.BlockSpec((1, d), lambda i, k: (0, 0)),
            pl.BlockSpec((1, d), lambda i, k: (0, 0)),
            pl.BlockSpec((1, d, tf), lambda i, k: (layer, 0, k)),
            pl.BlockSpec((1, d, tf), lambda i, k: (layer, 0, k)),
            pl.BlockSpec((1, tf, d), lambda i, k: (layer, k, 0)),
        ],
        out_specs=pl.BlockSpec((tm, d), lambda i, k: (i, 0)),
        out_shape=jax.ShapeDtypeStruct((t, d), F32),
        scratch_shapes=[pltpu.VMEM((tm, d), BF16)],
        compiler_params=_params(("parallel", "arbitrary")),
        name="ffn",
    )(x, gpre, gpost, wg, wu, wd)


def _kv_kernel(mem_ref, g_ref, wk_ref, wv_ref, kt_ref, v_ref):
    memn = _rms(mem_ref[0], g_ref[0]).astype(BF16)
    kt_ref[0, 0] = _dot(memn, wk_ref[0]).T.astype(BF16)
    v_ref[0, 0] = _dot(memn, wv_ref[0]).astype(BF16)


def _kv(mem, mem_norm, wk, wv):
    b, m, d = mem.shape
    n_layers = wk.shape[0]
    return pl.pallas_call(
        _kv_kernel,
        grid=(n_layers, b),
        in_specs=[
            pl.BlockSpec((1, m, d), lambda l, i: (i, 0, 0)),
            pl.BlockSpec((1, 1, d), lambda l, i: (l, 0, 0)),
            pl.BlockSpec((1, d, d), lambda l, i: (l, 0, 0)),
            pl.BlockSpec((1, d, d), lambda l, i: (l, 0, 0)),
        ],
        out_specs=[
            pl.BlockSpec((1, 1, d, m), lambda l, i: (l, i, 0, 0)),
            pl.BlockSpec((1, 1, m, d), lambda l, i: (l, i, 0, 0)),
        ],
        out_shape=[
            jax.ShapeDtypeStruct((n_layers, b, d, m), BF16),
            jax.ShapeDtypeStruct((n_layers, b, m, d), BF16),
        ],
        compiler_params=_params(("arbitrary", "parallel")),
        name="memory_kv",
    )(mem, mem_norm, wk, wv)


def _attn_kernel(x_ref, gpre_ref, gpost_ref, wq_ref, kt_ref, v_ref, wo_ref, *rest):
    n_cast = (len(rest) - 2) // 2
    cast_in, o_ref, cast_out, xn_ref = (rest[:n_cast], rest[n_cast], rest[n_cast + 1:-1],
                                        rest[-1])
    for src_ref, dst_ref in zip(cast_in, cast_out):
        dst_ref[...] = src_ref[...].astype(BF16)

    tm, d = x_ref.shape[1], x_ref.shape[2]
    hd = d // N_XHEADS
    gpre = gpre_ref[...]
    for r in _row_chunks(tm):
        xn_ref[r, :] = _rms(x_ref[0, r, :], gpre).astype(BF16)
    q = _dot(xn_ref[...], wq_ref[0]).astype(BF16)
    heads = []
    for h in range(N_XHEADS):
        cols = slice(h * hd, (h + 1) * hd)
        s = _dot(q[:, cols], kt_ref[0, 0, cols, :]) * (hd ** -0.5)
        s = s - jnp.max(s, axis=-1, keepdims=True)
        e = jnp.exp(s)
        p = (e / jnp.sum(e, axis=-1, keepdims=True)).astype(BF16)
        heads.append(_dot(p, v_ref[0, 0, :, cols]).astype(BF16))
    o = jnp.concatenate(heads, axis=-1)
    o_ref[0] = _dot(o, wo_ref[0])
    gpost = gpost_ref[...]
    for r in _row_chunks(tm):
        o_ref[0, r, :] = x_ref[0, r, :] + _rms(o_ref[0, r, :], gpost)


def _attn(x, gpre, gpost, wq, kt, v, wo, layer, *, tm, cast=()):
    b, s, d = x.shape
    m = v.shape[2]
    tiles_per_seq = s // tm
    assert b * tiles_per_seq >= CAST_STEPS

    def cast_block(i, j):
        return jnp.minimum(i * tiles_per_seq + j, CAST_STEPS - 1)

    cast_shapes = [(1, w.shape[1] // CAST_STEPS, w.shape[2]) for w in cast]
    outs = pl.pallas_call(
        _attn_kernel,
        grid=(b, tiles_per_seq),
        in_specs=[
            pl.BlockSpec((1, tm, d), lambda i, j: (i, j, 0)),
            _const_spec((1, d)),
            _const_spec((1, d)),
            _layer_spec((d, d), layer),
            pl.BlockSpec((1, 1, d, m), lambda i, j: (layer, i, 0, 0)),
            pl.BlockSpec((1, 1, m, d), lambda i, j: (layer, i, 0, 0)),
            _layer_spec((d, d), layer),
        ] + [pl.BlockSpec(blk, lambda i, j: (layer, cast_block(i, j), 0)) for blk in cast_shapes],
        out_specs=[pl.BlockSpec((1, tm, d), lambda i, j: (i, j, 0))] + [
            pl.BlockSpec(blk, lambda i, j: (0, cast_block(i, j), 0)) for blk in cast_shapes],
        out_shape=[jax.ShapeDtypeStruct((b, s, d), F32)] + [
            jax.ShapeDtypeStruct((1,) + w.shape[1:], BF16) for w in cast],
        scratch_shapes=[pltpu.VMEM((tm, d), BF16)],
        compiler_params=_params(("arbitrary", "arbitrary")),
        name="cross_attn",
    )(x, gpre, gpost, wq, kt, v, wo, *cast)
    return outs[0], outs[1:]


def _pool_kernel(*refs, seq_len, split):
    n_src = 1 if split is None else 2
    sources = [refs[3 * n:3 * n + 3] for n in range(n_src)]
    gpre_ref, gpost_ref, w_ref, scale_ref, o_ref, ext_ref, diff_ref = refs[3 * n_src:]
    i = pl.program_id(0)
    j = pl.program_id(1)
    tm, d = o_ref.shape[1], o_ref.shape[2]
    gd = d // len(POOL_WINDOWS)

    def for_active_source(fn):
        if split is None:
            fn(*sources[0])
        else:
            pl.when(i < split)(lambda: fn(*sources[0]))
            pl.when(i >= split)(lambda: fn(*sources[1]))

    def fill_ext(xprev_ref, x_ref, xnext_ref):
        gpre = gpre_ref[...]
        hprev = _rms(xprev_ref[0], gpre)
        hnext = _rms(xnext_ref[0], gpre)
        ext_ref[0:POOL_HALO, :] = jnp.where(j > 0, hprev, 0.0)
        for r in _row_chunks(tm):
            ext_ref[POOL_HALO + r.start:POOL_HALO + r.stop, :] = _rms(x_ref[0, r, :], gpre)
        ext_ref[POOL_HALO + tm:, :] = jnp.where(j < pl.num_programs(1) - 1, hnext, 0.0)

    for_active_source(fill_ext)

    n_ext = POOL_ROWS + 2 * POOL_HALO

    def shift_up(a, k):
        return pltpu.roll(a, n_ext - k, axis=0)

    for r0 in range(0, tm, POOL_ROWS):
        t = j * tm + r0 + lax.broadcasted_iota(jnp.int32, (POOL_ROWS, 1), 0)
        for g, win in enumerate(POOL_WINDOWS):
            half = win // 2
            cols = slice(g * gd, (g + 1) * gd)
            run = ext_ref[r0:r0 + n_ext, cols]
            width = 1
            while width < half:
                run = run + shift_up(run, width)
                width *= 2
            lo = run if half == POOL_HALO else shift_up(run, POOL_HALO - half)
            wsum = lo[:POOL_ROWS] + run[POOL_HALO:POOL_HALO + POOL_ROWS]
            count = (jnp.minimum(t + half, seq_len) - jnp.maximum(t - half, 0)).astype(F32)
            centre = ext_ref[POOL_HALO + r0:POOL_HALO + r0 + POOL_ROWS, cols]
            diff_ref[r0:r0 + POOL_ROWS, cols] = (wsum * (1.0 / count) - centre).astype(BF16)

    for g in range(len(POOL_WINDOWS)):
        cols = slice(g * gd, (g + 1) * gd)
        o_ref[0, :, cols] = _dot(diff_ref[:, cols], w_ref[0, g]) * scale_ref[:, cols]

    def add_residual(xprev_ref, x_ref, xnext_ref):
        gpost = gpost_ref[...]
        for r in _row_chunks(tm):
            o_ref[0, r, :] = x_ref[0, r, :] + _rms(o_ref[0, r, :], gpost)

    for_active_source(add_residual)


def _pool(xs, gpre, gpost, w, scale, layer, *, tm):
    s, d = xs[0].shape[1:]
    _, n_groups, gd, _ = w.shape
    n_tiles = s // tm
    halo_blocks_per_tile = tm // POOL_HALO
    n_halo_blocks = s // POOL_HALO
    split = xs[0].shape[0] if len(xs) == 2 else None
    b = sum(x.shape[0] for x in xs)

    def source_specs(n):
        def locate(i, j):
            if split is None:
                return i, j
            if n == 0:
                return jnp.minimum(i, split - 1), jnp.where(i < split, j, n_tiles - 1)
            return jnp.maximum(i - split, 0), jnp.where(i >= split, j, 0)

        def prev_map(i, j):
            bi, tj = locate(i, j)
            return bi, jnp.maximum(tj * halo_blocks_per_tile - 1, 0), 0

        def tile_map(i, j):
            bi, tj = locate(i, j)
            return bi, tj, 0

        def next_map(i, j):
            bi, tj = locate(i, j)
            return bi, jnp.minimum((tj + 1) * halo_blocks_per_tile, n_halo_blocks - 1), 0

        return [pl.BlockSpec((1, POOL_HALO, d), prev_map), pl.BlockSpec((1, tm, d), tile_map),
                pl.BlockSpec((1, POOL_HALO, d), next_map)]

    return pl.pallas_call(
        functools.partial(_pool_kernel, seq_len=s, split=split),
        grid=(b, n_tiles),
        in_specs=[spec for n in range(len(xs)) for spec in source_specs(n)] + [
            _const_spec((1, d)),
            _const_spec((1, d)),
            _layer_spec((n_groups, gd, gd), layer),
            _const_spec((1, d)),
        ],
        out_specs=pl.BlockSpec((1, tm, d), lambda i, j: (i, j, 0)),
        out_shape=jax.ShapeDtypeStruct((b, s, d), F32),
        scratch_shapes=[pltpu.VMEM((tm + 2 * POOL_HALO, d), F32), pltpu.VMEM((tm, d), BF16)],
        compiler_params=_params(("parallel", "parallel")),
        name="pool_mixer",
    )(*[x for x in xs for _ in range(3)], gpre, gpost, w, scale)


def _gmlp_kernel(x_ref, gpre_ref, gpost_ref, wu_ref, wv_ref, lng_ref, lnb_ref, ws_ref, bs_ref,
                 wout_ref, o_ref, xn_ref, u_ref, v_ref, vn_ref, mix_ref, gated_ref):
    tm, d = x_ref.shape[1], x_ref.shape[2]
    dg = u_ref.shape[1]
    hd = dg // N_SG_HEADS
    gpre = gpre_ref[...]
    for r in _row_chunks(tm):
        xn_ref[r, :] = _rms(x_ref[0, r, :], gpre).astype(BF16)
    xn = xn_ref[...]

    v_ref[...] = _dot(xn, wv_ref[0])
    u_ref[...] = _dot(xn, wu_ref[0])
    lng, lnb = lng_ref[...], lnb_ref[...]
    for r in _row_chunks(tm):
        v = _gelu_tanh(v_ref[r, :])
        vc = v - jnp.mean(v, axis=-1, keepdims=True)
        var = jnp.mean(vc * vc, axis=-1, keepdims=True)
        vn_ref[r, :] = (vc * lax.rsqrt(var + EPS) * lng + lnb).astype(BF16)

    for c in range(tm // CHUNK):
        rows = slice(c * CHUNK, (c + 1) * CHUNK)
        for h in range(N_SG_HEADS):
            cols = slice(h * hd, (h + 1) * hd)
            mix_ref[rows, cols] = _dot(ws_ref[0, h], vn_ref[rows, cols])

    for r in _row_chunks(tm):
        bias = bs_ref[0, r.start % CHUNK:r.start % CHUNK + ROW_CHUNK, :]
        gated_ref[r, :] = (_gelu_tanh(u_ref[r, :]) * (mix_ref[r, :] + bias)).astype(BF16)

    o_ref[0] = _dot(gated_ref[...], wout_ref[0])
    gpost = gpost_ref[...]
    for r in _row_chunks(tm):
        o_ref[0, r, :] = x_ref[0, r, :] + _rms(o_ref[0, r, :], gpost)


def _gmlp(x, gpre, gpost, w_in, lng, lnb, ws, bs_full, wout, layer, *, tm):
    b, s, d = x.shape
    dg = wout.shape[1]
    return pl.pallas_call(
        _gmlp_kernel,
        grid=(b, s // tm),
        in_specs=[
            pl.BlockSpec((1, tm, d), lambda i, j: (i, j, 0)),
            _const_spec((1, d)),
            _const_spec((1, d)),
            _layer_spec((d, dg), layer, col_block=0),
            _layer_spec((d, dg), layer, col_block=1),
            _const_spec((1, dg)),
            _const_spec((1, dg)),
            _layer_spec(ws.shape[1:], layer),
            _layer_spec(bs_full.shape[1:], layer),
            _layer_spec((dg, d), layer),
        ],
        out_specs=pl.BlockSpec((1, tm, d), lambda i, j: (i, j, 0)),
        out_shape=jax.ShapeDtypeStruct((b, s, d), F32),
        scratch_shapes=[
            pltpu.VMEM((tm, d), BF16),
            pltpu.VMEM((tm, dg), F32),
            pltpu.VMEM((tm, dg), F32),
            pltpu.VMEM((tm, dg), BF16),
            pltpu.VMEM((tm, dg), F32),
            pltpu.VMEM((tm, dg), BF16),
        ],
        compiler_params=_params(("parallel", "parallel")),
        name="gmlp_mixer",
    )(x, gpre, gpost, w_in, w_in, lng, lnb, ws, bs_full, wout)


def _trunk(xs, kt, v, p):
    s, d = xs[0].shape[1:]
    group_rows = [x.shape[0] * s for x in xs]
    b = sum(x.shape[0] for x in xs)
    depth = p["norm_gains"].shape[0]
    ffn_tiles = dict(tm=1024, tf=512)
    for i in range(depth):
        g = p["norm_gains"][i]
        gain = lambda n: g[n][None, :]
        j = i // 2
        if i % 2 == 0:
            x = _pool(xs if i == 0 else (x,), gain(0), gain(1), p["pool_w"],
                      p["pool_scale"][j][None, :], j, tm=512)
        else:
            x = _gmlp(x, gain(0), gain(1), p["gmlp_w_in"], p["gmlp_ln_g"][j][None, :],
                      p["gmlp_ln_b"][j][None, :], p["gmlp_w_s"], p["gmlp_bs_full"],
                      p["gmlp_w_out"], j, tm=256)
        x, ffn_w = _attn(x, gain(2), gain(3), p["attn_wq"], kt, v, p["attn_wo"], i, tm=512,
                         cast=(p["ffn_w_gate"], p["ffn_w_up"], p["ffn_w_down"]))
        ffn_args = (x.reshape(b * s, d), gain(4), gain(5), *ffn_w, 0)
        if i < depth - 1:
            x = _ffn(*ffn_args, **ffn_tiles).reshape(b, s, d)
    starts = np.cumsum([0] + group_rows[:-1])
    return tuple(_ffn(*ffn_args, **ffn_tiles, row_start=int(r0), n_rows=n).reshape(x_in.shape)
                 for x_in, r0, n in zip(xs, starts, group_rows))


def kernel(x_prompt, x_sample, mem_prompt, mem_sample, norm_gains, mem_norm, pool_w, pool_scale, gmlp_w_in, gmlp_ln_g, gmlp_ln_b, gmlp_w_s, gmlp_b_s, gmlp_w_out, attn_wq, attn_wk, attn_wv, attn_wo, ffn_w_gate, ffn_w_up, ffn_w_down):
    dg = gmlp_w_out.shape[1]
    hd = dg // N_SG_HEADS
    bs_full = jnp.repeat(jnp.swapaxes(gmlp_b_s, 1, 2), hd, axis=2)
    p = dict(
        norm_gains=norm_gains,
        pool_w=pool_w.astype(BF16), pool_scale=pool_scale,
        gmlp_w_in=gmlp_w_in.astype(BF16),
        gmlp_ln_g=gmlp_ln_g, gmlp_ln_b=gmlp_ln_b, gmlp_w_s=gmlp_w_s.astype(BF16),
        gmlp_bs_full=bs_full, gmlp_w_out=gmlp_w_out.astype(BF16),
        attn_wq=attn_wq.astype(BF16), attn_wo=attn_wo.astype(BF16),
        ffn_w_gate=ffn_w_gate, ffn_w_up=ffn_w_up, ffn_w_down=ffn_w_down,
    )
    wk = attn_wk.astype(BF16)
    wv = attn_wv.astype(BF16)
    kt, v = _kv(jnp.concatenate([mem_prompt, mem_sample], axis=0), mem_norm[:, None, :], wk, wv)
    return _trunk((x_prompt, x_sample), kt, v, p)
```

```python
import functools

import numpy as np
import jax
import jax.numpy as jnp
from jax import lax
from jax.experimental import pallas as pl
from jax.experimental.pallas import tpu as pltpu

F32 = jnp.float32
BF16 = jnp.bfloat16

EPS = 1e-6
POOL_WINDOWS = (2, 4, 8, 16)
POOL_HALO = max(POOL_WINDOWS) // 2
CHUNK = 128
N_SG_HEADS = 8
N_XHEADS = 4

V7X_VMEM_LIMIT_BYTES = 60 * 1024 * 1024
ROW_CHUNK = 16
POOL_ROWS = 64
CAST_STEPS = 32

GELU_C0 = np.float32(np.sqrt(2.0 / np.pi))
GELU_C1 = np.float32(0.044715)


def _rms(x, g):
    ms = jnp.mean(x * x, axis=-1, keepdims=True)
    return x * lax.rsqrt(ms + EPS) * g


def _gelu_tanh(x):
    return x * (0.5 * (1.0 + jnp.tanh(GELU_C0 * (x + GELU_C1 * (x * x * x)))))


def _dot(a, b):
    return jnp.dot(a, b, preferred_element_type=F32)


def _row_chunks(n_rows):
    return [slice(r, r + ROW_CHUNK) for r in range(0, n_rows, ROW_CHUNK)]


def _const_spec(shape):
    return pl.BlockSpec(shape, lambda *_: (0,) * len(shape), pipeline_mode=pl.Buffered(1))


def _layer_spec(shape, layer, col_block=0):
    index = (layer,) + (0,) * (len(shape) - 1) + (col_block,)
    return pl.BlockSpec((1,) + tuple(shape), lambda *_: index, pipeline_mode=pl.Buffered(1))


def _cast_plan(cast, step_of):
    blocks = [(1, w.shape[1] // CAST_STEPS, w.shape[2]) for w, _ in cast]
    in_specs = [pl.BlockSpec(blk, lambda *ids, layer=layer: (layer, step_of(*ids), 0))
                for blk, (_, layer) in zip(blocks, cast)]
    out_specs = [pl.BlockSpec(blk, lambda *ids: (0, step_of(*ids), 0)) for blk in blocks]
    out_shapes = [jax.ShapeDtypeStruct((1,) + w.shape[1:], BF16) for w, _ in cast]
    return in_specs, out_specs, out_shapes


def _cast_step_of(n_steps, tiles_per_seq):
    assert n_steps >= CAST_STEPS
    return lambda i, j: jnp.minimum(i * tiles_per_seq + j, CAST_STEPS - 1)


def _convert_blocks(src_refs, dst_refs):
    for src_ref, dst_ref in zip(src_refs, dst_refs):
        dst_ref[...] = src_ref[...].astype(BF16)


def _params(semantics):
    return pltpu.CompilerParams(dimension_semantics=semantics,
                                vmem_limit_bytes=V7X_VMEM_LIMIT_BYTES)


def _ffn_kernel(x_ref, gpre_ref, gpost_ref, wg_ref, wu_ref, wd_ref, o_ref, xn_ref):
    k = pl.program_id(1)
    tm = x_ref.shape[0]

    @pl.when(k == 0)
    def _():
        gpre = gpre_ref[...]
        for r in _row_chunks(tm):
            xn_ref[r, :] = _rms(x_ref[r, :], gpre).astype(BF16)
        o_ref[...] = jnp.zeros_like(o_ref)

    xn = xn_ref[...]
    gate = _dot(xn, wg_ref[0])
    up = _dot(xn, wu_ref[0])
    h = (gate * (1.0 / (1.0 + jnp.exp(-gate))) * up).astype(BF16)
    o_ref[...] += _dot(h, wd_ref[0])

    @pl.when(k == pl.num_programs(1) - 1)
    def _():
        gpost = gpost_ref[...]
        for r in _row_chunks(tm):
            o_ref[r, :] = x_ref[r, :] + _rms(o_ref[r, :], gpost)


def _ffn(x, gpre, gpost, wg, wu, wd, layer, *, tm, tf, row_start=0, n_rows=None):
    t, d = x.shape
    t = t if n_rows is None else n_rows
    first_tile = row_start // tm
    f = wg.shape[2]
    return pl.pallas_call(
        _ffn_kernel,
        grid=(t // tm, f // tf),
        in_specs=[
            pl.BlockSpec((tm, d), lambda i, k: (i + first_tile, 0)),
            pl.BlockSpec((1, d), lambda i, k: (0, 0)),
            pl.BlockSpec((1, d), lambda i, k: (0, 0)),
            pl.BlockSpec((1, d, tf), lambda i, k: (layer, 0, k)),
            pl.BlockSpec((1, d, tf), lambda i, k: (layer, 0, k)),
            pl.BlockSpec((1, tf, d), lambda i, k: (layer, k, 0)),
        ],
        out_specs=pl.BlockSpec((tm, d), lambda i, k: (i, 0)),
        out_shape=jax.ShapeDtypeStruct((t, d), F32),
        scratch_shapes=[pltpu.VMEM((tm, d), BF16)],
        compiler_params=_params(("parallel", "arbitrary")),
        name="ffn",
    )(x, gpre, gpost, wg, wu, wd)


def _kv_kernel(mem_ref, g_ref, wk_ref, wv_ref, kt_ref, v_ref):
    memn = _rms(mem_ref[0], g_ref[0]).astype(BF16)
    kt_ref[0, 0] = _dot(memn, wk_ref[0]).T.astype(BF16)
    v_ref[0, 0] = _dot(memn, wv_ref[0]).astype(BF16)


def _kv(mem, mem_norm, wk, wv):
    b, m, d = mem.shape
    n_layers = wk.shape[0]
    return pl.pallas_call(
        _kv_kernel,
        grid=(n_layers, b),
        in_specs=[
            pl.BlockSpec((1, m, d), lambda l, i: (i, 0, 0)),
            pl.BlockSpec((1, 1, d), lambda l, i: (l, 0, 0)),
            pl.BlockSpec((1, d, d), lambda l, i: (l, 0, 0)),
            pl.BlockSpec((1, d, d), lambda l, i: (l, 0, 0)),
        ],
        out_specs=[
            pl.BlockSpec((1, 1, d, m), lambda l, i: (l, i, 0, 0)),
            pl.BlockSpec((1, 1, m, d), lambda l, i: (l, i, 0, 0)),
        ],
        out_shape=[
            jax.ShapeDtypeStruct((n_layers, b, d, m), BF16),
            jax.ShapeDtypeStruct((n_layers, b, m, d), BF16),
        ],
        compiler_params=_params(("arbitrary", "parallel")),
        name="memory_kv",
    )(mem, mem_norm, wk, wv)


def _attn_kernel(x_ref, gpre_ref, gpost_ref, wq_ref, kt_ref, v_ref, wo_ref, *rest):
    n_cast = (len(rest) - 2) // 2
    cast_in, o_ref, cast_out, xn_ref = (rest[:n_cast], rest[n_cast], rest[n_cast + 1:-1],
                                        rest[-1])
    _convert_blocks(cast_in, cast_out)

    tm, d = x_ref.shape[1], x_ref.shape[2]
    hd = d // N_XHEADS
    gpre = gpre_ref[...]
    for r in _row_chunks(tm):
        xn_ref[r, :] = _rms(x_ref[0, r, :], gpre).astype(BF16)
    q = _dot(xn_ref[...], wq_ref[0]).astype(BF16)
    heads = []
    for h in range(N_XHEADS):
        cols = slice(h * hd, (h + 1) * hd)
        s = _dot(q[:, cols], kt_ref[0, 0, cols, :]) * (hd ** -0.5)
        s = s - jnp.max(s, axis=-1, keepdims=True)
        e = jnp.exp(s)
        p = (e / jnp.sum(e, axis=-1, keepdims=True)).astype(BF16)
        heads.append(_dot(p, v_ref[0, 0, :, cols]).astype(BF16))
    o = jnp.concatenate(heads, axis=-1)
    o_ref[0] = _dot(o, wo_ref[0])
    gpost = gpost_ref[...]
    for r in _row_chunks(tm):
        o_ref[0, r, :] = x_ref[0, r, :] + _rms(o_ref[0, r, :], gpost)


def _attn(x, gpre, gpost, wq, kt, v, wo, layer, *, tm, cast=()):
    b, s, d = x.shape
    m = v.shape[2]
    tiles_per_seq = s // tm
    cast_in, cast_out, cast_shapes = _cast_plan(cast, _cast_step_of(b * tiles_per_seq,
                                                                    tiles_per_seq))
    outs = pl.pallas_call(
        _attn_kernel,
        grid=(b, tiles_per_seq),
        in_specs=[
            pl.BlockSpec((1, tm, d), lambda i, j: (i, j, 0)),
            _const_spec((1, d)),
            _const_spec((1, d)),
            _layer_spec((d, d), 0),
            pl.BlockSpec((1, 1, d, m), lambda i, j: (layer, i, 0, 0)),
            pl.BlockSpec((1, 1, m, d), lambda i, j: (layer, i, 0, 0)),
            _layer_spec((d, d), 0),
        ] + cast_in,
        out_specs=[pl.BlockSpec((1, tm, d), lambda i, j: (i, j, 0))] + cast_out,
        out_shape=[jax.ShapeDtypeStruct((b, s, d), F32)] + cast_shapes,
        scratch_shapes=[pltpu.VMEM((tm, d), BF16)],
        compiler_params=_params(("arbitrary", "arbitrary")),
        name="cross_attn",
    )(x, gpre, gpost, wq, kt, v, wo, *[w for w, _ in cast])
    return outs[0], outs[1:]


def _pool_kernel(*refs, seq_len, split, n_cast):
    n_src = 1 if split is None else 2
    sources = [refs[3 * n:3 * n + 3] for n in range(n_src)]
    gpre_ref, gpost_ref, w_ref, scale_ref = refs[3 * n_src:3 * n_src + 4]
    cast_in = refs[3 * n_src + 4:3 * n_src + 4 + n_cast]
    o_ref, cast_out = refs[3 * n_src + 4 + n_cast], refs[3 * n_src + 5 + n_cast:-2]
    ext_ref, diff_ref = refs[-2:]
    _convert_blocks(cast_in, cast_out)
    i = pl.program_id(0)
    j = pl.program_id(1)
    tm, d = o_ref.shape[1], o_ref.shape[2]
    gd = d // len(POOL_WINDOWS)

    def for_active_source(fn):
        if split is None:
            fn(*sources[0])
        else:
            pl.when(i < split)(lambda: fn(*sources[0]))
            pl.when(i >= split)(lambda: fn(*sources[1]))

    def fill_ext(xprev_ref, x_ref, xnext_ref):
        gpre = gpre_ref[...]
        hprev = _rms(xprev_ref[0], gpre)
        hnext = _rms(xnext_ref[0], gpre)
        ext_ref[0:POOL_HALO, :] = jnp.where(j > 0, hprev, 0.0)
        for r in _row_chunks(tm):
            ext_ref[POOL_HALO + r.start:POOL_HALO + r.stop, :] = _rms(x_ref[0, r, :], gpre)
        ext_ref[POOL_HALO + tm:, :] = jnp.where(j < pl.num_programs(1) - 1, hnext, 0.0)

    for_active_source(fill_ext)

    n_ext = POOL_ROWS + 2 * POOL_HALO

    def shift_up(a, k):
        return pltpu.roll(a, n_ext - k, axis=0)

    for r0 in range(0, tm, POOL_ROWS):
        t = j * tm + r0 + lax.broadcasted_iota(jnp.int32, (POOL_ROWS, 1), 0)
        for g, win in enumerate(POOL_WINDOWS):
            half = win // 2
            cols = slice(g * gd, (g + 1) * gd)
            run = ext_ref[r0:r0 + n_ext, cols]
            width = 1
            while width < half:
                run = run + shift_up(run, width)
                width *= 2
            lo = run if half == POOL_HALO else shift_up(run, POOL_HALO - half)
            wsum = lo[:POOL_ROWS] + run[POOL_HALO:POOL_HALO + POOL_ROWS]
            count = (jnp.minimum(t + half, seq_len) - jnp.maximum(t - half, 0)).astype(F32)
            centre = ext_ref[POOL_HALO + r0:POOL_HALO + r0 + POOL_ROWS, cols]
            diff_ref[r0:r0 + POOL_ROWS, cols] = (wsum * (1.0 / count) - centre).astype(BF16)

    for g in range(len(POOL_WINDOWS)):
        cols = slice(g * gd, (g + 1) * gd)
        o_ref[0, :, cols] = _dot(diff_ref[:, cols], w_ref[0, g]) * scale_ref[:, cols]

    def add_residual(xprev_ref, x_ref, xnext_ref):
        gpost = gpost_ref[...]
        for r in _row_chunks(tm):
            o_ref[0, r, :] = x_ref[0, r, :] + _rms(o_ref[0, r, :], gpost)

    for_active_source(add_residual)


def _pool(xs, gpre, gpost, w, scale, layer, *, tm, cast=()):
    s, d = xs[0].shape[1:]
    _, n_groups, gd, _ = w.shape
    n_tiles = s // tm
    halo_blocks_per_tile = tm // POOL_HALO
    n_halo_blocks = s // POOL_HALO
    split = xs[0].shape[0] if len(xs) == 2 else None
    b = sum(x.shape[0] for x in xs)

    def source_specs(n):
        def locate(i, j):
            if split is None:
                return i, j
            if n == 0:
                return jnp.minimum(i, split - 1), jnp.where(i < split, j, n_tiles - 1)
            return jnp.maximum(i - split, 0), jnp.where(i >= split, j, 0)

        def prev_map(i, j):
            bi, tj = locate(i, j)
            return bi, jnp.maximum(tj * halo_blocks_per_tile - 1, 0), 0

        def tile_map(i, j):
            bi, tj = locate(i, j)
            return bi, tj, 0

        def next_map(i, j):
            bi, tj = locate(i, j)
            return bi, jnp.minimum((tj + 1) * halo_blocks_per_tile, n_halo_blocks - 1), 0

        return [pl.BlockSpec((1, POOL_HALO, d), prev_map), pl.BlockSpec((1, tm, d), tile_map),
                pl.BlockSpec((1, POOL_HALO, d), next_map)]

    cast_in, cast_out, cast_shapes = _cast_plan(cast, _cast_step_of(b * n_tiles, n_tiles))
    outs = pl.pallas_call(
        functools.partial(_pool_kernel, seq_len=s, split=split, n_cast=len(cast)),
        grid=(b, n_tiles),
        in_specs=[spec for n in range(len(xs)) for spec in source_specs(n)] + [
            _const_spec((1, d)),
            _const_spec((1, d)),
            _layer_spec((n_groups, gd, gd), layer),
            _const_spec((1, d)),
        ] + cast_in,
        out_specs=[pl.BlockSpec((1, tm, d), lambda i, j: (i, j, 0))] + cast_out,
        out_shape=[jax.ShapeDtypeStruct((b, s, d), F32)] + cast_shapes,
        scratch_shapes=[pltpu.VMEM((tm + 2 * POOL_HALO, d), F32), pltpu.VMEM((tm, d), BF16)],
        compiler_params=_params(("arbitrary", "arbitrary")),
        name="pool_mixer",
    )(*[x for x in xs for _ in range(3)], gpre, gpost, w, scale, *[w_ for w_, _ in cast])
    return outs[0], outs[1:]


def _gmlp_kernel(x_ref, gpre_ref, gpost_ref, wu_ref, wv_ref, lng_ref, lnb_ref, ws_ref, bs_ref,
                 wout_ref, *rest):
    n_cast = (len(rest) - 5) // 2
    cast_in, o_ref, cast_out = rest[:n_cast], rest[n_cast], rest[n_cast + 1:-4]
    xn_ref, u_ref, v_ref, vn_ref = rest[-4:]
    _convert_blocks(cast_in, cast_out)
    tm, d = x_ref.shape[1], x_ref.shape[2]
    dg = u_ref.shape[1]
    hd = dg // N_SG_HEADS
    mix_ref, gated_ref = v_ref, xn_ref
    gpre = gpre_ref[...]
    for r in _row_chunks(tm):
        xn_ref[r, :] = _rms(x_ref[0, r, :], gpre).astype(BF16)

    v_ref[...] = _dot(xn_ref[...], wv_ref[0])
    u_ref[...] = _dot(xn_ref[...], wu_ref[0])
    lng, lnb = lng_ref[...], lnb_ref[...]
    for r in _row_chunks(tm):
        v = _gelu_tanh(v_ref[r, :])
        vc = v - jnp.mean(v, axis=-1, keepdims=True)
        var = jnp.mean(vc * vc, axis=-1, keepdims=True)
        vn_ref[r, :] = (vc * lax.rsqrt(var + EPS) * lng + lnb).astype(BF16)

    for c in range(tm // CHUNK):
        rows = slice(c * CHUNK, (c + 1) * CHUNK)
        for h in range(N_SG_HEADS):
            cols = slice(h * hd, (h + 1) * hd)
            mix_ref[rows, cols] = _dot(ws_ref[0, h], vn_ref[rows, cols])

    for r in _row_chunks(tm):
        bias = bs_ref[0, r.start % CHUNK:r.start % CHUNK + ROW_CHUNK, :]
        gated_ref[r, :] = (_gelu_tanh(u_ref[r, :]) * (mix_ref[r, :] + bias)).astype(BF16)

    o_ref[0] = _dot(gated_ref[...], wout_ref[0])
    gpost = gpost_ref[...]
    for r in _row_chunks(tm):
        o_ref[0, r, :] = x_ref[0, r, :] + _rms(o_ref[0, r, :], gpost)


def _gmlp(x, gpre, gpost, w_in, lng, lnb, ws, bs_full, wout, layer, *, tm, cast=()):
    b, s, d = x.shape
    dg = wout.shape[1]
    assert dg == d
    tiles_per_seq = s // tm
    cast_in, cast_out, cast_shapes = _cast_plan(cast, _cast_step_of(b * tiles_per_seq,
                                                                    tiles_per_seq))
    outs = pl.pallas_call(
        _gmlp_kernel,
        grid=(b, tiles_per_seq),
        in_specs=[
            pl.BlockSpec((1, tm, d), lambda i, j: (i, j, 0)),
            _const_spec((1, d)),
            _const_spec((1, d)),
            _layer_spec((d, dg), 0, col_block=0),
            _layer_spec((d, dg), 0, col_block=1),
            _const_spec((1, dg)),
            _const_spec((1, dg)),
            _layer_spec(ws.shape[1:], layer),
            _layer_spec(bs_full.shape[1:], layer),
            _layer_spec((dg, d), 0),
        ] + cast_in,
        out_specs=[pl.BlockSpec((1, tm, d), lambda i, j: (i, j, 0))] + cast_out,
        out_shape=[jax.ShapeDtypeStruct((b, s, d), F32)] + cast_shapes,
        scratch_shapes=[
            pltpu.VMEM((tm, d), BF16),
            pltpu.VMEM((tm, dg), F32),
            pltpu.VMEM((tm, dg), F32),
            pltpu.VMEM((tm, dg), BF16),
        ],
        compiler_params=_params(("arbitrary", "arbitrary")),
        name="gmlp_mixer",
    )(x, gpre, gpost, w_in, w_in, lng, lnb, ws, bs_full, wout, *[w for w, _ in cast])
    return outs[0], outs[1:]


def _trunk(xs, kt, v, p):
    s, d = xs[0].shape[1:]
    group_rows = [x.shape[0] * s for x in xs]
    b = sum(x.shape[0] for x in xs)
    depth = p["norm_gains"].shape[0]
    ffn_tiles = dict(tm=1024, tf=512)
    for i in range(depth):
        g = p["norm_gains"][i]
        gain = lambda n: g[n][None, :]
        j = i // 2
        attn_cast = [(p["attn_wq"], i), (p["attn_wo"], i)]
        if i % 2 == 0:
            gmlp_cast = [(p["gmlp_w_in"], j), (p["gmlp_w_out"], j)] if i + 1 < depth else []
            x, (wq, wo, *gmlp_w) = _pool(xs if i == 0 else (x,), gain(0), gain(1), p["pool_w"],
                                         p["pool_scale"][j][None, :], j, tm=512,
                                         cast=attn_cast + gmlp_cast)
        else:
            x, (wq, wo) = _gmlp(x, gain(0), gain(1), gmlp_w[0], p["gmlp_ln_g"][j][None, :],
                                p["gmlp_ln_b"][j][None, :], p["gmlp_w_s"], p["gmlp_bs_full"],
                                gmlp_w[1], j, tm=512, cast=attn_cast)
        x, ffn_w = _attn(x, gain(2), gain(3), wq, kt, v, wo, i, tm=512,
                         cast=[(p[name], i) for name in ("ffn_w_gate", "ffn_w_up",
                                                         "ffn_w_down")])
        ffn_args = (x.reshape(b * s, d), gain(4), gain(5), *ffn_w, 0)
        if i < depth - 1:
            x = _ffn(*ffn_args, **ffn_tiles).reshape(b, s, d)
    starts = np.cumsum([0] + group_rows[:-1])
    return tuple(_ffn(*ffn_args, **ffn_tiles, row_start=int(r0), n_rows=n).reshape(x_in.shape)
                 for x_in, r0, n in zip(xs, starts, group_rows))


def kernel(x_prompt, x_sample, mem_prompt, mem_sample, norm_gains, mem_norm, pool_w, pool_scale, gmlp_w_in, gmlp_ln_g, gmlp_ln_b, gmlp_w_s, gmlp_b_s, gmlp_w_out, attn_wq, attn_wk, attn_wv, attn_wo, ffn_w_gate, ffn_w_up, ffn_w_down):
    dg = gmlp_w_out.shape[1]
    hd = dg // N_SG_HEADS
    bs_full = jnp.repeat(jnp.swapaxes(gmlp_b_s, 1, 2), hd, axis=2)
    p = dict(
        norm_gains=norm_gains,
        pool_w=pool_w.astype(BF16), pool_scale=pool_scale,
        gmlp_w_in=gmlp_w_in, gmlp_w_out=gmlp_w_out,
        gmlp_ln_g=gmlp_ln_g, gmlp_ln_b=gmlp_ln_b, gmlp_w_s=gmlp_w_s.astype(BF16),
        gmlp_bs_full=bs_full,
        attn_wq=attn_wq, attn_wo=attn_wo,
        ffn_w_gate=ffn_w_gate, ffn_w_up=ffn_w_up, ffn_w_down=ffn_w_down,
    )
    wk = attn_wk.astype(BF16)
    wv = attn_wv.astype(BF16)
    kt, v = _kv(jnp.concatenate([mem_prompt, mem_sample], axis=0), mem_norm[:, None, :], wk, wv)
    return _trunk((x_prompt, x_sample), kt, v, p)
```

```python
import functools

import numpy as np
import jax
import jax.numpy as jnp
from jax import lax
from jax.experimental import pallas as pl
from jax.experimental.pallas import tpu as pltpu

F32 = jnp.float32
BF16 = jnp.bfloat16

EPS = 1e-6
POOL_WINDOWS = (2, 4, 8, 16)
POOL_HALO = max(POOL_WINDOWS) // 2
CHUNK = 128
N_SG_HEADS = 8
N_XHEADS = 4

V7X_VMEM_LIMIT_BYTES = 60 * 1024 * 1024
ROW_CHUNK = 16
POOL_ROWS = 64
CAST_STEPS = 32

GELU_C0 = np.float32(np.sqrt(2.0 / np.pi))
GELU_C1 = np.float32(0.044715)


def _rms(x, g):
    ms = jnp.mean(x * x, axis=-1, keepdims=True)
    return x * lax.rsqrt(ms + EPS) * g


def _gelu_tanh(x):
    return x * (0.5 * (1.0 + jnp.tanh(GELU_C0 * (x + GELU_C1 * (x * x * x)))))


def _dot(a, b):
    return jnp.dot(a, b, preferred_element_type=F32)


def _row_chunks(n_rows):
    return [slice(r, r + ROW_CHUNK) for r in range(0, n_rows, ROW_CHUNK)]


def _const_spec(shape):
    return pl.BlockSpec(shape, lambda *_: (0,) * len(shape), pipeline_mode=pl.Buffered(1))


def _layer_spec(shape, layer, col_block=0):
    index = (layer,) + (0,) * (len(shape) - 1) + (col_block,)
    return pl.BlockSpec((1,) + tuple(shape), lambda *_: index, pipeline_mode=pl.Buffered(1))


def _cast_plan(cast, step_of):
    blocks = [(1, w.shape[1] // CAST_STEPS, w.shape[2]) for w, _ in cast]
    in_specs = [pl.BlockSpec(blk, lambda *ids, layer=layer: (layer, step_of(*ids), 0))
                for blk, (_, layer) in zip(blocks, cast)]
    out_specs = [pl.BlockSpec(blk, lambda *ids: (0, step_of(*ids), 0)) for blk in blocks]
    out_shapes = [jax.ShapeDtypeStruct((1,) + w.shape[1:], BF16) for w, _ in cast]
    return in_specs, out_specs, out_shapes


def _cast_step_of(n_steps, tiles_per_seq):
    assert n_steps >= CAST_STEPS
    return lambda i, j: jnp.minimum(i * tiles_per_seq + j, CAST_STEPS - 1)


def _convert_blocks(src_refs, dst_refs):
    for src_ref, dst_ref in zip(src_refs, dst_refs):
        dst_ref[...] = src_ref[...].astype(BF16)


def _params(semantics):
    return pltpu.CompilerParams(dimension_semantics=semantics,
                                vmem_limit_bytes=V7X_VMEM_LIMIT_BYTES)


def _ffn_kernel(x_ref, gpre_ref, gpost_ref, wg_ref, wu_ref, wd_ref, o_ref, xn_ref):
    k = pl.program_id(1)
    tm = x_ref.shape[0]

    def partial_down_projection():
        xn = xn_ref[...]
        gate = _dot(xn, wg_ref[0])
        up = _dot(xn, wu_ref[0])
        h = (gate * (1.0 / (1.0 + jnp.exp(-gate))) * up).astype(BF16)
        return _dot(h, wd_ref[0])

    @pl.when(k == 0)
    def _():
        gpre = gpre_ref[...]
        for r in _row_chunks(tm):
            xn_ref[r, :] = _rms(x_ref[r, :], gpre).astype(BF16)
        o_ref[...] = partial_down_projection()

    @pl.when(k > 0)
    def _():
        o_ref[...] += partial_down_projection()

    @pl.when(k == pl.num_programs(1) - 1)
    def _():
        gpost = gpost_ref[...]
        for r in _row_chunks(tm):
            o_ref[r, :] = x_ref[r, :] + _rms(o_ref[r, :], gpost)


def _ffn(x, gpre, gpost, wg, wu, wd, layer, *, tm, tf, row_start=0, n_rows=None):
    t, d = x.shape
    t = t if n_rows is None else n_rows
    first_tile = row_start // tm
    f = wg.shape[2]
    return pl.pallas_call(
        _ffn_kernel,
        grid=(t // tm, f // tf),
        in_specs=[
            pl.BlockSpec((tm, d), lambda i, k: (i + first_tile, 0)),
            pl.BlockSpec((1, d), lambda i, k: (0, 0)),
            pl.BlockSpec((1, d), lambda i, k: (0, 0)),
            pl.BlockSpec((1, d, tf), lambda i, k: (layer, 0, k)),
            pl.BlockSpec((1, d, tf), lambda i, k: (layer, 0, k)),
            pl.BlockSpec((1, tf, d), lambda i, k: (layer, k, 0)),
        ],
        out_specs=pl.BlockSpec((tm, d), lambda i, k: (i, 0)),
        out_shape=jax.ShapeDtypeStruct((t, d), F32),
        scratch_shapes=[pltpu.VMEM((tm, d), BF16)],
        compiler_params=_params(("parallel", "arbitrary")),
        name="ffn",
    )(x, gpre, gpost, wg, wu, wd)


def _kv_kernel(mem_ref, g_ref, wk_ref, wv_ref, kt_ref, v_ref):
    memn = _rms(mem_ref[0], g_ref[0]).astype(BF16)
    kt_ref[0, 0] = _dot(memn, wk_ref[0]).T.astype(BF16)
    v_ref[0, 0] = _dot(memn, wv_ref[0]).astype(BF16)


def _kv(mem, mem_norm, wk, wv):
    b, m, d = mem.shape
    n_layers = wk.shape[0]
    return pl.pallas_call(
        _kv_kernel,
        grid=(n_layers, b),
        in_specs=[
            pl.BlockSpec((1, m, d), lambda l, i: (i, 0, 0)),
            pl.BlockSpec((1, 1, d), lambda l, i: (l, 0, 0)),
            pl.BlockSpec((1, d, d), lambda l, i: (l, 0, 0)),
            pl.BlockSpec((1, d, d), lambda l, i: (l, 0, 0)),
        ],
        out_specs=[
            pl.BlockSpec((1, 1, d, m), lambda l, i: (l, i, 0, 0)),
            pl.BlockSpec((1, 1, m, d), lambda l, i: (l, i, 0, 0)),
        ],
        out_shape=[
            jax.ShapeDtypeStruct((n_layers, b, d, m), BF16),
            jax.ShapeDtypeStruct((n_layers, b, m, d), BF16),
        ],
        compiler_params=_params(("arbitrary", "parallel")),
        name="memory_kv",
    )(mem, mem_norm, wk, wv)


def _attn_kernel(x_ref, gpre_ref, gpost_ref, wq_ref, kt_ref, v_ref, wo_ref, *rest):
    n_cast = (len(rest) - 2) // 2
    cast_in, o_ref, cast_out, xn_ref = (rest[:n_cast], rest[n_cast], rest[n_cast + 1:-1],
                                        rest[-1])
    _convert_blocks(cast_in, cast_out)

    tm, d = x_ref.shape[1], x_ref.shape[2]
    hd = d // N_XHEADS
    gpre = gpre_ref[...]
    for r in _row_chunks(tm):
        xn_ref[r, :] = _rms(x_ref[0, r, :], gpre).astype(BF16)
    q = _dot(xn_ref[...], wq_ref[0]).astype(BF16)
    heads = []
    for h in range(N_XHEADS):
        cols = slice(h * hd, (h + 1) * hd)
        s = _dot(q[:, cols], kt_ref[0, 0, cols, :]) * (hd ** -0.5)
        s = s - jnp.max(s, axis=-1, keepdims=True)
        e = jnp.exp(s)
        p = (e / jnp.sum(e, axis=-1, keepdims=True)).astype(BF16)
        heads.append(_dot(p, v_ref[0, 0, :, cols]).astype(BF16))
    o = jnp.concatenate(heads, axis=-1)
    o_ref[0] = _dot(o, wo_ref[0])
    gpost = gpost_ref[...]
    for r in _row_chunks(tm):
        o_ref[0, r, :] = x_ref[0, r, :] + _rms(o_ref[0, r, :], gpost)


def _attn(x, gpre, gpost, wq, kt, v, wo, layer, *, tm, cast=()):
    b, s, d = x.shape
    m = v.shape[2]
    tiles_per_seq = s // tm
    cast_in, cast_out, cast_shapes = _cast_plan(cast, _cast_step_of(b * tiles_per_seq,
                                                                    tiles_per_seq))
    outs = pl.pallas_call(
        _attn_kernel,
        grid=(b, tiles_per_seq),
        in_specs=[
            pl.BlockSpec((1, tm, d), lambda i, j: (i, j, 0)),
            _const_spec((1, d)),
            _const_spec((1, d)),
            _layer_spec((d, d), 0),
            pl.BlockSpec((1, 1, d, m), lambda i, j: (layer, i, 0, 0)),
            pl.BlockSpec((1, 1, m, d), lambda i, j: (layer, i, 0, 0)),
            _layer_spec((d, d), 0),
        ] + cast_in,
        out_specs=[pl.BlockSpec((1, tm, d), lambda i, j: (i, j, 0))] + cast_out,
        out_shape=[jax.ShapeDtypeStruct((b, s, d), F32)] + cast_shapes,
        scratch_shapes=[pltpu.VMEM((tm, d), BF16)],
        compiler_params=_params(("arbitrary", "arbitrary")),
        name="cross_attn",
    )(x, gpre, gpost, wq, kt, v, wo, *[w for w, _ in cast])
    return outs[0], outs[1:]


def _pool_kernel(*refs, seq_len, split, n_cast):
    n_src = 1 if split is None else 2
    sources = [refs[3 * n:3 * n + 3] for n in range(n_src)]
    gpre_ref, gpost_ref, w_ref, scale_ref = refs[3 * n_src:3 * n_src + 4]
    cast_in = refs[3 * n_src + 4:3 * n_src + 4 + n_cast]
    o_ref, cast_out = refs[3 * n_src + 4 + n_cast], refs[3 * n_src + 5 + n_cast:-2]
    ext_ref, diff_ref = refs[-2:]
    _convert_blocks(cast_in, cast_out)
    i = pl.program_id(0)
    j = pl.program_id(1)
    tm, d = o_ref.shape[1], o_ref.shape[2]
    gd = d // len(POOL_WINDOWS)

    def for_active_source(fn):
        if split is None:
            fn(*sources[0])
        else:
            pl.when(i < split)(lambda: fn(*sources[0]))
            pl.when(i >= split)(lambda: fn(*sources[1]))

    def fill_ext(xprev_ref, x_ref, xnext_ref):
        gpre = gpre_ref[...]
        hprev = _rms(xprev_ref[0], gpre)
        hnext = _rms(xnext_ref[0], gpre)
        ext_ref[0:POOL_HALO, :] = jnp.where(j > 0, hprev, 0.0)
        for r in _row_chunks(tm):
            ext_ref[POOL_HALO + r.start:POOL_HALO + r.stop, :] = _rms(x_ref[0, r, :], gpre)
        ext_ref[POOL_HALO + tm:, :] = jnp.where(j < pl.num_programs(1) - 1, hnext, 0.0)

    for_active_source(fill_ext)

    n_ext = POOL_ROWS + 2 * POOL_HALO

    def shift_up(a, k):
        return pltpu.roll(a, n_ext - k, axis=0)

    for r0 in range(0, tm, POOL_ROWS):
        t = j * tm + r0 + lax.broadcasted_iota(jnp.int32, (POOL_ROWS, 1), 0)
        for g, win in enumerate(POOL_WINDOWS):
            half = win // 2
            cols = slice(g * gd, (g + 1) * gd)
            run = ext_ref[r0:r0 + n_ext, cols]
            width = 1
            while width < half:
                run = run + shift_up(run, width)
                width *= 2
            lo = run if half == POOL_HALO else shift_up(run, POOL_HALO - half)
            wsum = lo[:POOL_ROWS] + run[POOL_HALO:POOL_HALO + POOL_ROWS]
            count = (jnp.minimum(t + half, seq_len) - jnp.maximum(t - half, 0)).astype(F32)
            centre = ext_ref[POOL_HALO + r0:POOL_HALO + r0 + POOL_ROWS, cols]
            diff_ref[r0:r0 + POOL_ROWS, cols] = (wsum * (1.0 / count) - centre).astype(BF16)

    for g in range(len(POOL_WINDOWS)):
        cols = slice(g * gd, (g + 1) * gd)
        o_ref[0, :, cols] = _dot(diff_ref[:, cols], w_ref[0, g]) * scale_ref[:, cols]

    def add_residual(xprev_ref, x_ref, xnext_ref):
        gpost = gpost_ref[...]
        for r in _row_chunks(tm):
            o_ref[0, r, :] = x_ref[0, r, :] + _rms(o_ref[0, r, :], gpost)

    for_active_source(add_residual)


def _pool(xs, gpre, gpost, w, scale, layer, *, tm, cast=()):
    s, d = xs[0].shape[1:]
    _, n_groups, gd, _ = w.shape
    n_tiles = s // tm
    halo_blocks_per_tile = tm // POOL_HALO
    n_halo_blocks = s // POOL_HALO
    split = xs[0].shape[0] if len(xs) == 2 else None
    b = sum(x.shape[0] for x in xs)

    def source_specs(n):
        def locate(i, j):
            if split is None:
                return i, j
            if n == 0:
                return jnp.minimum(i, split - 1), jnp.where(i < split, j, n_tiles - 1)
            return jnp.maximum(i - split, 0), jnp.where(i >= split, j, 0)

        def prev_map(i, j):
            bi, tj = locate(i, j)
            return bi, jnp.maximum(tj * halo_blocks_per_tile - 1, 0), 0

        def tile_map(i, j):
            bi, tj = locate(i, j)
            return bi, tj, 0

        def next_map(i, j):
            bi, tj = locate(i, j)
            return bi, jnp.minimum((tj + 1) * halo_blocks_per_tile, n_halo_blocks - 1), 0

        return [pl.BlockSpec((1, POOL_HALO, d), prev_map), pl.BlockSpec((1, tm, d), tile_map),
                pl.BlockSpec((1, POOL_HALO, d), next_map)]

    cast_in, cast_out, cast_shapes = _cast_plan(cast, _cast_step_of(b * n_tiles, n_tiles))
    outs = pl.pallas_call(
        functools.partial(_pool_kernel, seq_len=s, split=split, n_cast=len(cast)),
        grid=(b, n_tiles),
        in_specs=[spec for n in range(len(xs)) for spec in source_specs(n)] + [
            _const_spec((1, d)),
            _const_spec((1, d)),
            _layer_spec((n_groups, gd, gd), layer),
            _const_spec((1, d)),
        ] + cast_in,
        out_specs=[pl.BlockSpec((1, tm, d), lambda i, j: (i, j, 0))] + cast_out,
        out_shape=[jax.ShapeDtypeStruct((b, s, d), F32)] + cast_shapes,
        scratch_shapes=[pltpu.VMEM((tm + 2 * POOL_HALO, d), F32), pltpu.VMEM((tm, d), BF16)],
        compiler_params=_params(("arbitrary", "arbitrary")),
        name="pool_mixer",
    )(*[x for x in xs for _ in range(3)], gpre, gpost, w, scale, *[w_ for w_, _ in cast])
    return outs[0], outs[1:]


def _gmlp_kernel(x_ref, gpre_ref, gpost_ref, wu_ref, wv_ref, lng_ref, lnb_ref, ws_ref, bs_ref,
                 wout_ref, *rest):
    n_cast = (len(rest) - 5) // 2
    cast_in, o_ref, cast_out = rest[:n_cast], rest[n_cast], rest[n_cast + 1:-4]
    xn_ref, u_ref, v_ref, vn_ref = rest[-4:]
    _convert_blocks(cast_in, cast_out)
    tm, d = x_ref.shape[1], x_ref.shape[2]
    dg = u_ref.shape[1]
    hd = dg // N_SG_HEADS
    mix_ref, gated_ref = v_ref, xn_ref
    gpre = gpre_ref[...]
    for r in _row_chunks(tm):
        xn_ref[r, :] = _rms(x_ref[0, r, :], gpre).astype(BF16)

    v_ref[...] = _dot(xn_ref[...], wv_ref[0])
    u_ref[...] = _dot(xn_ref[...], wu_ref[0])
    lng, lnb = lng_ref[...], lnb_ref[...]
    for r in _row_chunks(tm):
        v = _gelu_tanh(v_ref[r, :])
        vc = v - jnp.mean(v, axis=-1, keepdims=True)
        var = jnp.mean(vc * vc, axis=-1, keepdims=True)
        vn_ref[r, :] = (vc * lax.rsqrt(var + EPS) * lng + lnb).astype(BF16)

    for c in range(tm // CHUNK):
        rows = slice(c * CHUNK, (c + 1) * CHUNK)
        for h in range(N_SG_HEADS):
            cols = slice(h * hd, (h + 1) * hd)
            mix_ref[rows, cols] = _dot(ws_ref[0, h], vn_ref[rows, cols])

    for r in _row_chunks(tm):
        bias = bs_ref[0, r.start % CHUNK:r.start % CHUNK + ROW_CHUNK, :]
        gated_ref[r, :] = (_gelu_tanh(u_ref[r, :]) * (mix_ref[r, :] + bias)).astype(BF16)

    o_ref[0] = _dot(gated_ref[...], wout_ref[0])
    gpost = gpost_ref[...]
    for r in _row_chunks(tm):
        o_ref[0, r, :] = x_ref[0, r, :] + _rms(o_ref[0, r, :], gpost)


def _gmlp(x, gpre, gpost, w_in, lng, lnb, ws, bs_full, wout, layer, *, tm, cast=()):
    b, s, d = x.shape
    dg = wout.shape[1]
    assert dg == d
    tiles_per_seq = s // tm
    cast_in, cast_out, cast_shapes = _cast_plan(cast, _cast_step_of(b * tiles_per_seq,
                                                                    tiles_per_seq))
    outs = pl.pallas_call(
        _gmlp_kernel,
        grid=(b, tiles_per_seq),
        in_specs=[
            pl.BlockSpec((1, tm, d), lambda i, j: (i, j, 0)),
            _const_spec((1, d)),
            _const_spec((1, d)),
            _layer_spec((d, dg), 0, col_block=0),
            _layer_spec((d, dg), 0, col_block=1),
            _const_spec((1, dg)),
            _const_spec((1, dg)),
            _layer_spec(ws.shape[1:], layer),
            _layer_spec(bs_full.shape[1:], layer),
            _layer_spec((dg, d), 0),
        ] + cast_in,
        out_specs=[pl.BlockSpec((1, tm, d), lambda i, j: (i, j, 0))] + cast_out,
        out_shape=[jax.ShapeDtypeStruct((b, s, d), F32)] + cast_shapes,
        scratch_shapes=[
            pltpu.VMEM((tm, d), BF16),
            pltpu.VMEM((tm, dg), F32),
            pltpu.VMEM((tm, dg), F32),
            pltpu.VMEM((tm, dg), BF16),
        ],
        compiler_params=_params(("arbitrary", "arbitrary")),
        name="gmlp_mixer",
    )(x, gpre, gpost, w_in, w_in, lng, lnb, ws, bs_full, wout, *[w for w, _ in cast])
    return outs[0], outs[1:]


def _trunk(xs, kt, v, p):
    s, d = xs[0].shape[1:]
    group_rows = [x.shape[0] * s for x in xs]
    b = sum(x.shape[0] for x in xs)
    depth = p["norm_gains"].shape[0]
    ffn_tiles = dict(tm=1024, tf=512)
    for i in range(depth):
        g = p["norm_gains"][i]
        gain = lambda n: g[n][None, :]
        j = i // 2
        attn_cast = [(p["attn_wq"], i), (p["attn_wo"], i)]
        if i % 2 == 0:
            gmlp_cast = [(p["gmlp_w_in"], j), (p["gmlp_w_out"], j)] if i + 1 < depth else []
            x, (wq, wo, *gmlp_w) = _pool(xs if i == 0 else (x,), gain(0), gain(1), p["pool_w"],
                                         p["pool_scale"][j][None, :], j, tm=512,
                                         cast=attn_cast + gmlp_cast)
        else:
            x, (wq, wo) = _gmlp(x, gain(0), gain(1), gmlp_w[0], p["gmlp_ln_g"][j][None, :],
                                p["gmlp_ln_b"][j][None, :], p["gmlp_w_s"], p["gmlp_bs_full"],
                                gmlp_w[1], j, tm=512, cast=attn_cast)
        x, ffn_w = _attn(x, gain(2), gain(3), wq, kt, v, wo, i, tm=512,
                         cast=[(p[name], i) for name in ("ffn_w_gate", "ffn_w_up",
                                                         "ffn_w_down")])
        ffn_args = (x.reshape(b * s, d), gain(4), gain(5), *ffn_w, 0)
        if i < depth - 1:
            x = _ffn(*ffn_args, **ffn_tiles).reshape(b, s, d)
    starts = np.cumsum([0] + group_rows[:-1])
    return tuple(_ffn(*ffn_args, **ffn_tiles, row_start=int(r0), n_rows=n).reshape(x_in.shape)
                 for x_in, r0, n in zip(xs, starts, group_rows))


def kernel(x_prompt, x_sample, mem_prompt, mem_sample, norm_gains, mem_norm, pool_w, pool_scale, gmlp_w_in, gmlp_ln_g, gmlp_ln_b, gmlp_w_s, gmlp_b_s, gmlp_w_out, attn_wq, attn_wk, attn_wv, attn_wo, ffn_w_gate, ffn_w_up, ffn_w_down):
    dg = gmlp_w_out.shape[1]
    hd = dg // N_SG_HEADS
    n_layers, n_heads, chunk = gmlp_b_s.shape
    bs_full = jnp.broadcast_to(jnp.swapaxes(gmlp_b_s, 1, 2)[..., None],
                               (n_layers, chunk, n_heads, hd)).reshape(n_layers, chunk, dg)
    p = dict(
        norm_gains=norm_gains,
        pool_w=pool_w.astype(BF16), pool_scale=pool_scale,
        gmlp_w_in=gmlp_w_in, gmlp_w_out=gmlp_w_out,
        gmlp_ln_g=gmlp_ln_g, gmlp_ln_b=gmlp_ln_b, gmlp_w_s=gmlp_w_s.astype(BF16),
        gmlp_bs_full=bs_full,
        attn_wq=attn_wq, attn_wo=attn_wo,
        ffn_w_gate=ffn_w_gate, ffn_w_up=ffn_w_up, ffn_w_down=ffn_w_down,
    )
    wk = attn_wk.astype(BF16)
    wv = attn_wv.astype(BF16)
    kt, v = _kv(jnp.concatenate([mem_prompt, mem_sample], axis=0), mem_norm[:, None, :], wk, wv)
    return _trunk((x_prompt, x_sample), kt, v, p)
```

```python
import functools

import numpy as np
import jax
import jax.numpy as jnp
from jax import lax
from jax.experimental import pallas as pl
from jax.experimental.pallas import tpu as pltpu

F32 = jnp.float32
BF16 = jnp.bfloat16

EPS = 1e-6
POOL_WINDOWS = (2, 4, 8, 16)
POOL_HALO = max(POOL_WINDOWS) // 2
CHUNK = 128
N_SG_HEADS = 8
N_XHEADS = 4

V7X_VMEM_LIMIT_BYTES = 60 * 1024 * 1024
ROW_CHUNK = 16
POOL_ROWS = 64
CAST_STEPS = 32
FFN_WEIGHT_SLOTS = 3

GELU_C0 = np.float32(np.sqrt(2.0 / np.pi))
GELU_C1 = np.float32(0.044715)


def _rms(x, g):
    ms = jnp.mean(x * x, axis=-1, keepdims=True)
    return x * lax.rsqrt(ms + EPS) * g


def _gelu_tanh(x):
    return x * (0.5 * (1.0 + jnp.tanh(GELU_C0 * (x + GELU_C1 * (x * x * x)))))


def _dot(a, b):
    return jnp.dot(a, b, preferred_element_type=F32)


def _row_chunks(n_rows):
    return [slice(r, r + ROW_CHUNK) for r in range(0, n_rows, ROW_CHUNK)]


def _const_spec(shape):
    return pl.BlockSpec(shape, lambda *_: (0,) * len(shape), pipeline_mode=pl.Buffered(1))


def _layer_spec(shape, layer, col_block=0):
    index = (layer,) + (0,) * (len(shape) - 1) + (col_block,)
    return pl.BlockSpec((1,) + tuple(shape), lambda *_: index, pipeline_mode=pl.Buffered(1))


def _cast_plan(cast, step_of):
    blocks = [(1, w.shape[1] // CAST_STEPS, w.shape[2]) for w, _ in cast]
    in_specs = [pl.BlockSpec(blk, lambda *ids, layer=layer: (layer, step_of(*ids), 0))
                for blk, (_, layer) in zip(blocks, cast)]
    out_specs = [pl.BlockSpec(blk, lambda *ids: (0, step_of(*ids), 0)) for blk in blocks]
    out_shapes = [jax.ShapeDtypeStruct((1,) + w.shape[1:], BF16) for w, _ in cast]
    return in_specs, out_specs, out_shapes


def _cast_step_of(n_steps, tiles_per_seq):
    assert n_steps >= CAST_STEPS
    return lambda i, j: jnp.minimum(i * tiles_per_seq + j, CAST_STEPS - 1)


def _convert_blocks(src_refs, dst_refs):
    for src_ref, dst_ref in zip(src_refs, dst_refs):
        dst_ref[...] = src_ref[...].astype(BF16)


def _params(semantics):
    return pltpu.CompilerParams(dimension_semantics=semantics,
                                vmem_limit_bytes=V7X_VMEM_LIMIT_BYTES)


def _ffn_kernel(x_ref, gpre_ref, gpost_ref, wg_hbm, wu_hbm, wd_hbm, o_ref,
                xn_ref, wg_buf, wu_buf, wd_buf, sems, *, layer, tf):
    i = pl.program_id(0)
    tm = x_ref.shape[0]
    n_chunks = wg_hbm.shape[2] // tf

    def chunk_copies(k):
        slot = k % FFN_WEIGHT_SLOTS
        cols = pl.ds(k * tf, tf)
        return (
            pltpu.make_async_copy(wg_hbm.at[layer, :, cols], wg_buf.at[slot], sems.at[slot, 0]),
            pltpu.make_async_copy(wu_hbm.at[layer, :, cols], wu_buf.at[slot], sems.at[slot, 1]),
            pltpu.make_async_copy(wd_hbm.at[layer, cols, :], wd_buf.at[slot], sems.at[slot, 2]),
        )

    def start(k):
        for copy in chunk_copies(k):
            copy.start()

    def wait(k):
        for copy in chunk_copies(k):
            copy.wait()

    @pl.when(i == 0)
    def _():
        start(0)

    gpre = gpre_ref[...]
    for r in _row_chunks(tm):
        xn_ref[r, :] = _rms(x_ref[r, :], gpre).astype(BF16)

    for k in range(n_chunks):
        start((k + 1) % n_chunks)
        wait(k)
        slot = k % FFN_WEIGHT_SLOTS
        xn = xn_ref[...]
        gate = _dot(xn, wg_buf[slot])
        up = _dot(xn, wu_buf[slot])
        h = (gate * (1.0 / (1.0 + jnp.exp(-gate))) * up).astype(BF16)
        if k == 0:
            o_ref[...] = _dot(h, wd_buf[slot])
        else:
            o_ref[...] += _dot(h, wd_buf[slot])

    gpost = gpost_ref[...]
    for r in _row_chunks(tm):
        o_ref[r, :] = x_ref[r, :] + _rms(o_ref[r, :], gpost)

    @pl.when(i == pl.num_programs(0) - 1)
    def _():
        wait(0)


def _ffn(x, gpre, gpost, wg, wu, wd, layer, *, tm, tf, row_start=0, n_rows=None):
    t, d = x.shape
    t = t if n_rows is None else n_rows
    first_tile = row_start // tm
    assert wg.shape[2] // tf >= FFN_WEIGHT_SLOTS
    return pl.pallas_call(
        functools.partial(_ffn_kernel, layer=layer, tf=tf),
        grid=(t // tm,),
        in_specs=[
            pl.BlockSpec((tm, d), lambda i: (i + first_tile, 0)),
            pl.BlockSpec((1, d), lambda i: (0, 0)),
            pl.BlockSpec((1, d), lambda i: (0, 0)),
            pl.BlockSpec(memory_space=pl.ANY),
            pl.BlockSpec(memory_space=pl.ANY),
            pl.BlockSpec(memory_space=pl.ANY),
        ],
        out_specs=pl.BlockSpec((tm, d), lambda i: (i, 0)),
        out_shape=jax.ShapeDtypeStruct((t, d), F32),
        scratch_shapes=[
            pltpu.VMEM((tm, d), BF16),
            pltpu.VMEM((FFN_WEIGHT_SLOTS, d, tf), BF16),
            pltpu.VMEM((FFN_WEIGHT_SLOTS, d, tf), BF16),
            pltpu.VMEM((FFN_WEIGHT_SLOTS, tf, d), BF16),
            pltpu.SemaphoreType.DMA((FFN_WEIGHT_SLOTS, 3)),
        ],
        compiler_params=_params(("arbitrary",)),
        name="ffn",
    )(x, gpre, gpost, wg, wu, wd)


def _kv_kernel(mem_ref, g_ref, wk_ref, wv_ref, kt_ref, v_ref):
    memn = _rms(mem_ref[0], g_ref[0]).astype(BF16)
    kt_ref[0, 0] = _dot(memn, wk_ref[0]).T.astype(BF16)
    v_ref[0, 0] = _dot(memn, wv_ref[0]).astype(BF16)


def _kv(mem, mem_norm, wk, wv):
    b, m, d = mem.shape
    n_layers = wk.shape[0]
    return pl.pallas_call(
        _kv_kernel,
        grid=(n_layers, b),
        in_specs=[
            pl.BlockSpec((1, m, d), lambda l, i: (i, 0, 0)),
            pl.BlockSpec((1, 1, d), lambda l, i: (l, 0, 0)),
            pl.BlockSpec((1, d, d), lambda l, i: (l, 0, 0)),
            pl.BlockSpec((1, d, d), lambda l, i: (l, 0, 0)),
        ],
        out_specs=[
            pl.BlockSpec((1, 1, d, m), lambda l, i: (l, i, 0, 0)),
            pl.BlockSpec((1, 1, m, d), lambda l, i: (l, i, 0, 0)),
        ],
        out_shape=[
            jax.ShapeDtypeStruct((n_layers, b, d, m), BF16),
            jax.ShapeDtypeStruct((n_layers, b, m, d), BF16),
        ],
        compiler_params=_params(("arbitrary", "parallel")),
        name="memory_kv",
    )(mem, mem_norm, wk, wv)


def _attn_kernel(x_ref, gpre_ref, gpost_ref, wq_ref, kt_ref, v_ref, wo_ref, *rest):
    n_cast = (len(rest) - 2) // 2
    cast_in, o_ref, cast_out, xn_ref = (rest[:n_cast], rest[n_cast], rest[n_cast + 1:-1],
                                        rest[-1])
    _convert_blocks(cast_in, cast_out)

    tm, d = x_ref.shape[1], x_ref.shape[2]
    hd = d // N_XHEADS
    gpre = gpre_ref[...]
    for r in _row_chunks(tm):
        xn_ref[r, :] = _rms(x_ref[0, r, :], gpre).astype(BF16)
    q = _dot(xn_ref[...], wq_ref[0]).astype(BF16)
    heads = []
    for h in range(N_XHEADS):
        cols = slice(h * hd, (h + 1) * hd)
        s = _dot(q[:, cols], kt_ref[0, 0, cols, :]) * (hd ** -0.5)
        s = s - jnp.max(s, axis=-1, keepdims=True)
        e = jnp.exp(s)
        p = (e / jnp.sum(e, axis=-1, keepdims=True)).astype(BF16)
        heads.append(_dot(p, v_ref[0, 0, :, cols]).astype(BF16))
    o = jnp.concatenate(heads, axis=-1)
    o_ref[0] = _dot(o, wo_ref[0])
    gpost = gpost_ref[...]
    for r in _row_chunks(tm):
        o_ref[0, r, :] = x_ref[0, r, :] + _rms(o_ref[0, r, :], gpost)


def _attn(x, gpre, gpost, wq, kt, v, wo, layer, *, tm, cast=()):
    b, s, d = x.shape
    m = v.shape[2]
    tiles_per_seq = s // tm
    cast_in, cast_out, cast_shapes = _cast_plan(cast, _cast_step_of(b * tiles_per_seq,
                                                                    tiles_per_seq))
    outs = pl.pallas_call(
        _attn_kernel,
        grid=(b, tiles_per_seq),
        in_specs=[
            pl.BlockSpec((1, tm, d), lambda i, j: (i, j, 0)),
            _const_spec((1, d)),
            _const_spec((1, d)),
            _layer_spec((d, d), 0),
            pl.BlockSpec((1, 1, d, m), lambda i, j: (layer, i, 0, 0)),
            pl.BlockSpec((1, 1, m, d), lambda i, j: (layer, i, 0, 0)),
            _layer_spec((d, d), 0),
        ] + cast_in,
        out_specs=[pl.BlockSpec((1, tm, d), lambda i, j: (i, j, 0))] + cast_out,
        out_shape=[jax.ShapeDtypeStruct((b, s, d), F32)] + cast_shapes,
        scratch_shapes=[pltpu.VMEM((tm, d), BF16)],
        compiler_params=_params(("arbitrary", "arbitrary")),
        name="cross_attn",
    )(x, gpre, gpost, wq, kt, v, wo, *[w for w, _ in cast])
    return outs[0], outs[1:]


def _pool_kernel(*refs, seq_len, split, n_cast):
    n_src = 1 if split is None else 2
    sources = [refs[3 * n:3 * n + 3] for n in range(n_src)]
    gpre_ref, gpost_ref, w_ref, scale_ref = refs[3 * n_src:3 * n_src + 4]
    cast_in = refs[3 * n_src + 4:3 * n_src + 4 + n_cast]
    o_ref, cast_out = refs[3 * n_src + 4 + n_cast], refs[3 * n_src + 5 + n_cast:-2]
    ext_ref, diff_ref = refs[-2:]
    _convert_blocks(cast_in, cast_out)
    i = pl.program_id(0)
    j = pl.program_id(1)
    tm, d = o_ref.shape[1], o_ref.shape[2]
    gd = d // len(POOL_WINDOWS)

    def for_active_source(fn):
        if split is None:
            fn(*sources[0])
        else:
            pl.when(i < split)(lambda: fn(*sources[0]))
            pl.when(i >= split)(lambda: fn(*sources[1]))

    def fill_ext(xprev_ref, x_ref, xnext_ref):
        gpre = gpre_ref[...]
        hprev = _rms(xprev_ref[0], gpre)
        hnext = _rms(xnext_ref[0], gpre)
        ext_ref[0:POOL_HALO, :] = jnp.where(j > 0, hprev, 0.0)
        for r in _row_chunks(tm):
            ext_ref[POOL_HALO + r.start:POOL_HALO + r.stop, :] = _rms(x_ref[0, r, :], gpre)
        ext_ref[POOL_HALO + tm:, :] = jnp.where(j < pl.num_programs(1) - 1, hnext, 0.0)

    for_active_source(fill_ext)

    n_ext = POOL_ROWS + 2 * POOL_HALO

    def shift_up(a, k):
        return pltpu.roll(a, n_ext - k, axis=0)

    for r0 in range(0, tm, POOL_ROWS):
        t = j * tm + r0 + lax.broadcasted_iota(jnp.int32, (POOL_ROWS, 1), 0)
        for g, win in enumerate(POOL_WINDOWS):
            half = win // 2
            cols = slice(g * gd, (g + 1) * gd)
            run = ext_ref[r0:r0 + n_ext, cols]
            width = 1
            while width < half:
                run = run + shift_up(run, width)
                width *= 2
            lo = run if half == POOL_HALO else shift_up(run, POOL_HALO - half)
            wsum = lo[:POOL_ROWS] + run[POOL_HALO:POOL_HALO + POOL_ROWS]
            count = (jnp.minimum(t + half, seq_len) - jnp.maximum(t - half, 0)).astype(F32)
            centre = ext_ref[POOL_HALO + r0:POOL_HALO + r0 + POOL_ROWS, cols]
            diff_ref[r0:r0 + POOL_ROWS, cols] = (wsum * (1.0 / count) - centre).astype(BF16)

    for g in range(len(POOL_WINDOWS)):
        cols = slice(g * gd, (g + 1) * gd)
        o_ref[0, :, cols] = _dot(diff_ref[:, cols], w_ref[0, g]) * scale_ref[:, cols]

    def add_residual(xprev_ref, x_ref, xnext_ref):
        gpost = gpost_ref[...]
        for r in _row_chunks(tm):
            o_ref[0, r, :] = x_ref[0, r, :] + _rms(o_ref[0, r, :], gpost)

    for_active_source(add_residual)


def _pool(xs, gpre, gpost, w, scale, layer, *, tm, cast=()):
    s, d = xs[0].shape[1:]
    _, n_groups, gd, _ = w.shape
    n_tiles = s // tm
    halo_blocks_per_tile = tm // POOL_HALO
    n_halo_blocks = s // POOL_HALO
    split = xs[0].shape[0] if len(xs) == 2 else None
    b = sum(x.shape[0] for x in xs)

    def source_specs(n):
        def locate(i, j):
            if split is None:
                return i, j
            if n == 0:
                return jnp.minimum(i, split - 1), jnp.where(i < split, j, n_tiles - 1)
            return jnp.maximum(i - split, 0), jnp.where(i >= split, j, 0)

        def prev_map(i, j):
            bi, tj = locate(i, j)
            return bi, jnp.maximum(tj * halo_blocks_per_tile - 1, 0), 0

        def tile_map(i, j):
            bi, tj = locate(i, j)
            return bi, tj, 0

        def next_map(i, j):
            bi, tj = locate(i, j)
            return bi, jnp.minimum((tj + 1) * halo_blocks_per_tile, n_halo_blocks - 1), 0

        return [pl.BlockSpec((1, POOL_HALO, d), prev_map), pl.BlockSpec((1, tm, d), tile_map),
                pl.BlockSpec((1, POOL_HALO, d), next_map)]

    cast_in, cast_out, cast_shapes = _cast_plan(cast, _cast_step_of(b * n_tiles, n_tiles))
    outs = pl.pallas_call(
        functools.partial(_pool_kernel, seq_len=s, split=split, n_cast=len(cast)),
        grid=(b, n_tiles),
        in_specs=[spec for n in range(len(xs)) for spec in source_specs(n)] + [
            _const_spec((1, d)),
            _const_spec((1, d)),
            _layer_spec((n_groups, gd, gd), layer),
            _const_spec((1, d)),
        ] + cast_in,
        out_specs=[pl.BlockSpec((1, tm, d), lambda i, j: (i, j, 0))] + cast_out,
        out_shape=[jax.ShapeDtypeStruct((b, s, d), F32)] + cast_shapes,
        scratch_shapes=[pltpu.VMEM((tm + 2 * POOL_HALO, d), F32), pltpu.VMEM((tm, d), BF16)],
        compiler_params=_params(("arbitrary", "arbitrary")),
        name="pool_mixer",
    )(*[x for x in xs for _ in range(3)], gpre, gpost, w, scale, *[w_ for w_, _ in cast])
    return outs[0], outs[1:]


def _gmlp_kernel(x_ref, gpre_ref, gpost_ref, wu_ref, wv_ref, lng_ref, lnb_ref, ws_ref, bs_ref,
                 wout_ref, *rest):
    n_cast = (len(rest) - 5) // 2
    cast_in, o_ref, cast_out = rest[:n_cast], rest[n_cast], rest[n_cast + 1:-4]
    xn_ref, u_ref, v_ref, vn_ref = rest[-4:]
    _convert_blocks(cast_in, cast_out)
    tm, d = x_ref.shape[1], x_ref.shape[2]
    dg = u_ref.shape[1]
    hd = dg // N_SG_HEADS
    mix_ref, gated_ref = v_ref, xn_ref
    gpre = gpre_ref[...]
    for r in _row_chunks(tm):
        xn_ref[r, :] = _rms(x_ref[0, r, :], gpre).astype(BF16)

    v_ref[...] = _dot(xn_ref[...], wv_ref[0])
    u_ref[...] = _dot(xn_ref[...], wu_ref[0])
    lng, lnb = lng_ref[...], lnb_ref[...]
    for r in _row_chunks(tm):
        v = _gelu_tanh(v_ref[r, :])
        vc = v - jnp.mean(v, axis=-1, keepdims=True)
        var = jnp.mean(vc * vc, axis=-1, keepdims=True)
        vn_ref[r, :] = (vc * lax.rsqrt(var + EPS) * lng + lnb).astype(BF16)

    for c in range(tm // CHUNK):
        rows = slice(c * CHUNK, (c + 1) * CHUNK)
        for h in range(N_SG_HEADS):
            cols = slice(h * hd, (h + 1) * hd)
            mix_ref[rows, cols] = _dot(ws_ref[0, h], vn_ref[rows, cols])

    for r in _row_chunks(tm):
        bias = bs_ref[0, r.start % CHUNK:r.start % CHUNK + ROW_CHUNK, :]
        gated_ref[r, :] = (_gelu_tanh(u_ref[r, :]) * (mix_ref[r, :] + bias)).astype(BF16)

    o_ref[0] = _dot(gated_ref[...], wout_ref[0])
    gpost = gpost_ref[...]
    for r in _row_chunks(tm):
        o_ref[0, r, :] = x_ref[0, r, :] + _rms(o_ref[0, r, :], gpost)


def _gmlp(x, gpre, gpost, w_in, lng, lnb, ws, bs_full, wout, layer, *, tm, cast=()):
    b, s, d = x.shape
    dg = wout.shape[1]
    assert dg == d
    tiles_per_seq = s // tm
    cast_in, cast_out, cast_shapes = _cast_plan(cast, _cast_step_of(b * tiles_per_seq,
                                                                    tiles_per_seq))
    outs = pl.pallas_call(
        _gmlp_kernel,
        grid=(b, tiles_per_seq),
        in_specs=[
            pl.BlockSpec((1, tm, d), lambda i, j: (i, j, 0)),
            _const_spec((1, d)),
            _const_spec((1, d)),
            _layer_spec((d, dg), 0, col_block=0),
            _layer_spec((d, dg), 0, col_block=1),
            _const_spec((1, dg)),
            _const_spec((1, dg)),
            _layer_spec(ws.shape[1:], layer),
            _layer_spec(bs_full.shape[1:], layer),
            _layer_spec((dg, d), 0),
        ] + cast_in,
        out_specs=[pl.BlockSpec((1, tm, d), lambda i, j: (i, j, 0))] + cast_out,
        out_shape=[jax.ShapeDtypeStruct((b, s, d), F32)] + cast_shapes,
        scratch_shapes=[
            pltpu.VMEM((tm, d), BF16),
            pltpu.VMEM((tm, dg), F32),
            pltpu.VMEM((tm, dg), F32),
            pltpu.VMEM((tm, dg), BF16),
        ],
        compiler_params=_params(("arbitrary", "arbitrary")),
        name="gmlp_mixer",
    )(x, gpre, gpost, w_in, w_in, lng, lnb, ws, bs_full, wout, *[w for w, _ in cast])
    return outs[0], outs[1:]


def _trunk(xs, kt, v, p):
    s, d = xs[0].shape[1:]
    group_rows = [x.shape[0] * s for x in xs]
    b = sum(x.shape[0] for x in xs)
    depth = p["norm_gains"].shape[0]
    ffn_tiles = dict(tm=1024, tf=512)
    for i in range(depth):
        g = p["norm_gains"][i]
        gain = lambda n: g[n][None, :]
        j = i // 2
        attn_cast = [(p["attn_wq"], i), (p["attn_wo"], i)]
        if i % 2 == 0:
            gmlp_cast = [(p["gmlp_w_in"], j), (p["gmlp_w_out"], j)] if i + 1 < depth else []
            x, (wq, wo, *gmlp_w) = _pool(xs if i == 0 else (x,), gain(0), gain(1), p["pool_w"],
                                         p["pool_scale"][j][None, :], j, tm=512,
                                         cast=attn_cast + gmlp_cast)
        else:
            x, (wq, wo) = _gmlp(x, gain(0), gain(1), gmlp_w[0], p["gmlp_ln_g"][j][None, :],
                                p["gmlp_ln_b"][j][None, :], p["gmlp_w_s"], p["gmlp_bs_full"],
                                gmlp_w[1], j, tm=512, cast=attn_cast)
        x, ffn_w = _attn(x, gain(2), gain(3), wq, kt, v, wo, i, tm=512,
                         cast=[(p[name], i) for name in ("ffn_w_gate", "ffn_w_up",
                                                         "ffn_w_down")])
        ffn_args = (x.reshape(b * s, d), gain(4), gain(5), *ffn_w, 0)
        if i < depth - 1:
            x = _ffn(*ffn_args, **ffn_tiles).reshape(b, s, d)
    starts = np.cumsum([0] + group_rows[:-1])
    return tuple(_ffn(*ffn_args, **ffn_tiles, row_start=int(r0), n_rows=n).reshape(x_in.shape)
                 for x_in, r0, n in zip(xs, starts, group_rows))


def kernel(x_prompt, x_sample, mem_prompt, mem_sample, norm_gains, mem_norm, pool_w, pool_scale, gmlp_w_in, gmlp_ln_g, gmlp_ln_b, gmlp_w_s, gmlp_b_s, gmlp_w_out, attn_wq, attn_wk, attn_wv, attn_wo, ffn_w_gate, ffn_w_up, ffn_w_down):
    dg = gmlp_w_out.shape[1]
    hd = dg // N_SG_HEADS
    n_layers, n_heads, chunk = gmlp_b_s.shape
    bs_full = jnp.broadcast_to(jnp.swapaxes(gmlp_b_s, 1, 2)[..., None],
                               (n_layers, chunk, n_heads, hd)).reshape(n_layers, chunk, dg)
    p = dict(
        norm_gains=norm_gains,
        pool_w=pool_w.astype(BF16), pool_scale=pool_scale,
        gmlp_w_in=gmlp_w_in, gmlp_w_out=gmlp_w_out,
        gmlp_ln_g=gmlp_ln_g, gmlp_ln_b=gmlp_ln_b, gmlp_w_s=gmlp_w_s.astype(BF16),
        gmlp_bs_full=bs_full,
        attn_wq=attn_wq, attn_wo=attn_wo,
        ffn_w_gate=ffn_w_gate, ffn_w_up=ffn_w_up, ffn_w_down=ffn_w_down,
    )
    wk = attn_wk.astype(BF16)
    wv = attn_wv.astype(BF16)
    kt, v = _kv(jnp.concatenate([mem_prompt, mem_sample], axis=0), mem_norm[:, None, :], wk, wv)
    return _trunk((x_prompt, x_sample), kt, v, p)
```

```python
import functools

import numpy as np
import jax
import jax.numpy as jnp
from jax import lax
from jax.experimental import pallas as pl
from jax.experimental.pallas import tpu as pltpu

F32 = jnp.float32
BF16 = jnp.bfloat16

EPS = 1e-6
POOL_WINDOWS = (2, 4, 8, 16)
POOL_HALO = max(POOL_WINDOWS) // 2
CHUNK = 128
N_SG_HEADS = 8
N_XHEADS = 4

V7X_VMEM_LIMIT_BYTES = 60 * 1024 * 1024
ROW_CHUNK = 16
POOL_ROWS = 64
CAST_STEPS = 32
FFN_CHUNK = 512

GELU_C0 = np.float32(np.sqrt(2.0 / np.pi))
GELU_C1 = np.float32(0.044715)


def _rms(x, g):
    ms = jnp.mean(x * x, axis=-1, keepdims=True)
    return x * lax.rsqrt(ms + EPS) * g


def _gelu_tanh(x):
    return x * (0.5 * (1.0 + jnp.tanh(GELU_C0 * (x + GELU_C1 * (x * x * x)))))


def _dot(a, b):
    return jnp.dot(a, b, preferred_element_type=F32)


def _row_chunks(n_rows):
    return [slice(r, r + ROW_CHUNK) for r in range(0, n_rows, ROW_CHUNK)]


def _const_spec(shape):
    return pl.BlockSpec(shape, lambda *_: (0,) * len(shape), pipeline_mode=pl.Buffered(1))


def _layer_spec(shape, layer, col_block=0):
    index = (layer,) + (0,) * (len(shape) - 1) + (col_block,)
    return pl.BlockSpec((1,) + tuple(shape), lambda *_: index, pipeline_mode=pl.Buffered(1))


def _cast_plan(cast, step_of):
    in_specs, out_specs, out_shapes = [], [], []
    for w, layer, chunk in cast:
        _, rows, cols = w.shape
        blk_rows = rows // CAST_STEPS
        in_specs.append(pl.BlockSpec((1, blk_rows, cols),
                                     lambda *ids, layer=layer: (layer, step_of(*ids), 0)))
        if chunk is None:
            out_specs.append(pl.BlockSpec((1, blk_rows, cols),
                                          lambda *ids: (0, step_of(*ids), 0)))
            out_shapes.append(jax.ShapeDtypeStruct((1, rows, cols), BF16))
        else:
            out_specs.append(pl.BlockSpec((1, cols // chunk, blk_rows, chunk),
                                          lambda *ids: (0, 0, step_of(*ids), 0)))
            out_shapes.append(jax.ShapeDtypeStruct((1, cols // chunk, rows, chunk), BF16))
    return in_specs, out_specs, out_shapes


def _cast_step_of(n_steps, tiles_per_seq):
    assert n_steps >= CAST_STEPS
    return lambda i, j: jnp.minimum(i * tiles_per_seq + j, CAST_STEPS - 1)


def _convert_blocks(src_refs, dst_refs):
    for src_ref, dst_ref in zip(src_refs, dst_refs):
        if len(dst_ref.shape) == 3:
            dst_ref[...] = src_ref[...].astype(BF16)
        else:
            chunk = dst_ref.shape[3]
            for c in range(dst_ref.shape[1]):
                dst_ref[0, c] = src_ref[0, :, c * chunk:(c + 1) * chunk].astype(BF16)


def _params(semantics):
    return pltpu.CompilerParams(dimension_semantics=semantics,
                                vmem_limit_bytes=V7X_VMEM_LIMIT_BYTES)


def _ffn_kernel(x_ref, gpre_ref, gpost_ref, wg_ref, wu_ref, wd_ref, o_ref, xn_ref):
    k = pl.program_id(1)
    tm = x_ref.shape[0]

    def partial_down_projection():
        xn = xn_ref[...]
        gate = _dot(xn, wg_ref[0, 0])
        up = _dot(xn, wu_ref[0, 0])
        h = (gate * (1.0 / (1.0 + jnp.exp(-gate))) * up).astype(BF16)
        return _dot(h, wd_ref[0])

    @pl.when(k == 0)
    def _():
        gpre = gpre_ref[...]
        for r in _row_chunks(tm):
            xn_ref[r, :] = _rms(x_ref[r, :], gpre).astype(BF16)
        o_ref[...] = partial_down_projection()

    @pl.when(k > 0)
    def _():
        o_ref[...] += partial_down_projection()

    @pl.when(k == pl.num_programs(1) - 1)
    def _():
        gpost = gpost_ref[...]
        for r in _row_chunks(tm):
            o_ref[r, :] = x_ref[r, :] + _rms(o_ref[r, :], gpost)


def _ffn(x, gpre, gpost, wg, wu, wd, layer, *, tm, row_start=0, n_rows=None):
    t, d = x.shape
    t = t if n_rows is None else n_rows
    first_tile = row_start // tm
    n_chunks, tf = wg.shape[1], wg.shape[3]
    return pl.pallas_call(
        _ffn_kernel,
        grid=(t // tm, n_chunks),
        in_specs=[
            pl.BlockSpec((tm, d), lambda i, k: (i + first_tile, 0)),
            pl.BlockSpec((1, d), lambda i, k: (0, 0)),
            pl.BlockSpec((1, d), lambda i, k: (0, 0)),
            pl.BlockSpec((1, 1, d, tf), lambda i, k: (layer, k, 0, 0)),
            pl.BlockSpec((1, 1, d, tf), lambda i, k: (layer, k, 0, 0)),
            pl.BlockSpec((1, tf, d), lambda i, k: (layer, k, 0)),
        ],
        out_specs=pl.BlockSpec((tm, d), lambda i, k: (i, 0)),
        out_shape=jax.ShapeDtypeStruct((t, d), F32),
        scratch_shapes=[pltpu.VMEM((tm, d), BF16)],
        compiler_params=_params(("parallel", "arbitrary")),
        name="ffn",
    )(x, gpre, gpost, wg, wu, wd)


def _kv_kernel(mem_ref, g_ref, wk_ref, wv_ref, kt_ref, v_ref):
    memn = _rms(mem_ref[0], g_ref[0]).astype(BF16)
    kt_ref[0, 0] = _dot(memn, wk_ref[0]).T.astype(BF16)
    v_ref[0, 0] = _dot(memn, wv_ref[0]).astype(BF16)


def _kv(mem, mem_norm, wk, wv):
    b, m, d = mem.shape
    n_layers = wk.shape[0]
    return pl.pallas_call(
        _kv_kernel,
        grid=(n_layers, b),
        in_specs=[
            pl.BlockSpec((1, m, d), lambda l, i: (i, 0, 0)),
            pl.BlockSpec((1, 1, d), lambda l, i: (l, 0, 0)),
            pl.BlockSpec((1, d, d), lambda l, i: (l, 0, 0)),
            pl.BlockSpec((1, d, d), lambda l, i: (l, 0, 0)),
        ],
        out_specs=[
            pl.BlockSpec((1, 1, d, m), lambda l, i: (l, i, 0, 0)),
            pl.BlockSpec((1, 1, m, d), lambda l, i: (l, i, 0, 0)),
        ],
        out_shape=[
            jax.ShapeDtypeStruct((n_layers, b, d, m), BF16),
            jax.ShapeDtypeStruct((n_layers, b, m, d), BF16),
        ],
        compiler_params=_params(("arbitrary", "parallel")),
        name="memory_kv",
    )(mem, mem_norm, wk, wv)


def _attn_kernel(x_ref, gpre_ref, gpost_ref, wq_ref, kt_ref, v_ref, wo_ref, *rest):
    n_cast = (len(rest) - 2) // 2
    cast_in, o_ref, cast_out, xn_ref = (rest[:n_cast], rest[n_cast], rest[n_cast + 1:-1],
                                        rest[-1])
    _convert_blocks(cast_in, cast_out)

    tm, d = x_ref.shape[1], x_ref.shape[2]
    hd = d // N_XHEADS
    gpre = gpre_ref[...]
    for r in _row_chunks(tm):
        xn_ref[r, :] = _rms(x_ref[0, r, :], gpre).astype(BF16)
    q = _dot(xn_ref[...], wq_ref[0]).astype(BF16)
    heads = []
    for h in range(N_XHEADS):
        cols = slice(h * hd, (h + 1) * hd)
        s = _dot(q[:, cols], kt_ref[0, 0, cols, :]) * (hd ** -0.5)
        s = s - jnp.max(s, axis=-1, keepdims=True)
        e = jnp.exp(s)
        p = (e / jnp.sum(e, axis=-1, keepdims=True)).astype(BF16)
        heads.append(_dot(p, v_ref[0, 0, :, cols]).astype(BF16))
    o = jnp.concatenate(heads, axis=-1)
    o_ref[0] = _dot(o, wo_ref[0])
    gpost = gpost_ref[...]
    for r in _row_chunks(tm):
        o_ref[0, r, :] = x_ref[0, r, :] + _rms(o_ref[0, r, :], gpost)


def _attn(x, gpre, gpost, wq, kt, v, wo, layer, *, tm, cast=()):
    b, s, d = x.shape
    m = v.shape[2]
    tiles_per_seq = s // tm
    cast_in, cast_out, cast_shapes = _cast_plan(cast, _cast_step_of(b * tiles_per_seq,
                                                                    tiles_per_seq))
    outs = pl.pallas_call(
        _attn_kernel,
        grid=(b, tiles_per_seq),
        in_specs=[
            pl.BlockSpec((1, tm, d), lambda i, j: (i, j, 0)),
            _const_spec((1, d)),
            _const_spec((1, d)),
            _layer_spec((d, d), 0),
            pl.BlockSpec((1, 1, d, m), lambda i, j: (layer, i, 0, 0)),
            pl.BlockSpec((1, 1, m, d), lambda i, j: (layer, i, 0, 0)),
            _layer_spec((d, d), 0),
        ] + cast_in,
        out_specs=[pl.BlockSpec((1, tm, d), lambda i, j: (i, j, 0))] + cast_out,
        out_shape=[jax.ShapeDtypeStruct((b, s, d), F32)] + cast_shapes,
        scratch_shapes=[pltpu.VMEM((tm, d), BF16)],
        compiler_params=_params(("arbitrary", "arbitrary")),
        name="cross_attn",
    )(x, gpre, gpost, wq, kt, v, wo, *[c[0] for c in cast])
    return outs[0], outs[1:]


def _pool_kernel(*refs, seq_len, split, n_cast):
    n_src = 1 if split is None else 2
    sources = [refs[3 * n:3 * n + 3] for n in range(n_src)]
    gpre_ref, gpost_ref, w_ref, scale_ref = refs[3 * n_src:3 * n_src + 4]
    cast_in = refs[3 * n_src + 4:3 * n_src + 4 + n_cast]
    o_ref, cast_out = refs[3 * n_src + 4 + n_cast], refs[3 * n_src + 5 + n_cast:-2]
    ext_ref, diff_ref = refs[-2:]
    _convert_blocks(cast_in, cast_out)
    i = pl.program_id(0)
    j = pl.program_id(1)
    tm, d = o_ref.shape[1], o_ref.shape[2]
    gd = d // len(POOL_WINDOWS)

    def for_active_source(fn):
        if split is None:
            fn(*sources[0])
        else:
            pl.when(i < split)(lambda: fn(*sources[0]))
            pl.when(i >= split)(lambda: fn(*sources[1]))

    def fill_ext(xprev_ref, x_ref, xnext_ref):
        gpre = gpre_ref[...]
        hprev = _rms(xprev_ref[0], gpre)
        hnext = _rms(xnext_ref[0], gpre)
        ext_ref[0:POOL_HALO, :] = jnp.where(j > 0, hprev, 0.0)
        for r in _row_chunks(tm):
            ext_ref[POOL_HALO + r.start:POOL_HALO + r.stop, :] = _rms(x_ref[0, r, :], gpre)
        ext_ref[POOL_HALO + tm:, :] = jnp.where(j < pl.num_programs(1) - 1, hnext, 0.0)

    for_active_source(fill_ext)

    n_ext = POOL_ROWS + 2 * POOL_HALO

    def shift_up(a, k):
        return pltpu.roll(a, n_ext - k, axis=0)

    for r0 in range(0, tm, POOL_ROWS):
        t = j * tm + r0 + lax.broadcasted_iota(jnp.int32, (POOL_ROWS, 1), 0)
        for g, win in enumerate(POOL_WINDOWS):
            half = win // 2
            cols = slice(g * gd, (g + 1) * gd)
            run = ext_ref[r0:r0 + n_ext, cols]
            width = 1
            while width < half:
                run = run + shift_up(run, width)
                width *= 2
            lo = run if half == POOL_HALO else shift_up(run, POOL_HALO - half)
            wsum = lo[:POOL_ROWS] + run[POOL_HALO:POOL_HALO + POOL_ROWS]
            count = (jnp.minimum(t + half, seq_len) - jnp.maximum(t - half, 0)).astype(F32)
            centre = ext_ref[POOL_HALO + r0:POOL_HALO + r0 + POOL_ROWS, cols]
            diff_ref[r0:r0 + POOL_ROWS, cols] = (wsum * (1.0 / count) - centre).astype(BF16)

    for g in range(len(POOL_WINDOWS)):
        cols = slice(g * gd, (g + 1) * gd)
        o_ref[0, :, cols] = _dot(diff_ref[:, cols], w_ref[0, g]) * scale_ref[:, cols]

    def add_residual(xprev_ref, x_ref, xnext_ref):
        gpost = gpost_ref[...]
        for r in _row_chunks(tm):
            o_ref[0, r, :] = x_ref[0, r, :] + _rms(o_ref[0, r, :], gpost)

    for_active_source(add_residual)


def _pool(xs, gpre, gpost, w, scale, layer, *, tm, cast=()):
    s, d = xs[0].shape[1:]
    _, n_groups, gd, _ = w.shape
    n_tiles = s // tm
    halo_blocks_per_tile = tm // POOL_HALO
    n_halo_blocks = s // POOL_HALO
    split = xs[0].shape[0] if len(xs) == 2 else None
    b = sum(x.shape[0] for x in xs)

    def source_specs(n):
        def locate(i, j):
            if split is None:
                return i, j
            if n == 0:
                return jnp.minimum(i, split - 1), jnp.where(i < split, j, n_tiles - 1)
            return jnp.maximum(i - split, 0), jnp.where(i >= split, j, 0)

        def prev_map(i, j):
            bi, tj = locate(i, j)
            return bi, jnp.maximum(tj * halo_blocks_per_tile - 1, 0), 0

        def tile_map(i, j):
            bi, tj = locate(i, j)
            return bi, tj, 0

        def next_map(i, j):
            bi, tj = locate(i, j)
            return bi, jnp.minimum((tj + 1) * halo_blocks_per_tile, n_halo_blocks - 1), 0

        return [pl.BlockSpec((1, POOL_HALO, d), prev_map), pl.BlockSpec((1, tm, d), tile_map),
                pl.BlockSpec((1, POOL_HALO, d), next_map)]

    cast_in, cast_out, cast_shapes = _cast_plan(cast, _cast_step_of(b * n_tiles, n_tiles))
    outs = pl.pallas_call(
        functools.partial(_pool_kernel, seq_len=s, split=split, n_cast=len(cast)),
        grid=(b, n_tiles),
        in_specs=[spec for n in range(len(xs)) for spec in source_specs(n)] + [
            _const_spec((1, d)),
            _const_spec((1, d)),
            _layer_spec((n_groups, gd, gd), layer),
            _const_spec((1, d)),
        ] + cast_in,
        out_specs=[pl.BlockSpec((1, tm, d), lambda i, j: (i, j, 0))] + cast_out,
        out_shape=[jax.ShapeDtypeStruct((b, s, d), F32)] + cast_shapes,
        scratch_shapes=[pltpu.VMEM((tm + 2 * POOL_HALO, d), F32), pltpu.VMEM((tm, d), BF16)],
        compiler_params=_params(("arbitrary", "arbitrary")),
        name="pool_mixer",
    )(*[x for x in xs for _ in range(3)], gpre, gpost, w, scale, *[c[0] for c in cast])
    return outs[0], outs[1:]


def _gmlp_kernel(x_ref, gpre_ref, gpost_ref, wu_ref, wv_ref, lng_ref, lnb_ref, ws_ref, bs_ref,
                 wout_ref, *rest):
    n_cast = (len(rest) - 5) // 2
    cast_in, o_ref, cast_out = rest[:n_cast], rest[n_cast], rest[n_cast + 1:-4]
    xn_ref, u_ref, v_ref, vn_ref = rest[-4:]
    _convert_blocks(cast_in, cast_out)
    tm, d = x_ref.shape[1], x_ref.shape[2]
    dg = u_ref.shape[1]
    hd = dg // N_SG_HEADS
    mix_ref, gated_ref = v_ref, xn_ref
    gpre = gpre_ref[...]
    for r in _row_chunks(tm):
        xn_ref[r, :] = _rms(x_ref[0, r, :], gpre).astype(BF16)

    v_ref[...] = _dot(xn_ref[...], wv_ref[0])
    u_ref[...] = _dot(xn_ref[...], wu_ref[0])
    lng, lnb = lng_ref[...], lnb_ref[...]
    for r in _row_chunks(tm):
        v = _gelu_tanh(v_ref[r, :])
        vc = v - jnp.mean(v, axis=-1, keepdims=True)
        var = jnp.mean(vc * vc, axis=-1, keepdims=True)
        vn_ref[r, :] = (vc * lax.rsqrt(var + EPS) * lng + lnb).astype(BF16)

    for c in range(tm // CHUNK):
        rows = slice(c * CHUNK, (c + 1) * CHUNK)
        for h in range(N_SG_HEADS):
            cols = slice(h * hd, (h + 1) * hd)
            mix_ref[rows, cols] = _dot(ws_ref[0, h], vn_ref[rows, cols])

    for r in _row_chunks(tm):
        bias = bs_ref[0, r.start % CHUNK:r.start % CHUNK + ROW_CHUNK, :]
        gated_ref[r, :] = (_gelu_tanh(u_ref[r, :]) * (mix_ref[r, :] + bias)).astype(BF16)

    o_ref[0] = _dot(gated_ref[...], wout_ref[0])
    gpost = gpost_ref[...]
    for r in _row_chunks(tm):
        o_ref[0, r, :] = x_ref[0, r, :] + _rms(o_ref[0, r, :], gpost)


def _gmlp(x, gpre, gpost, w_in, lng, lnb, ws, bs_full, wout, layer, *, tm, cast=()):
    b, s, d = x.shape
    dg = wout.shape[1]
    assert dg == d
    tiles_per_seq = s // tm
    cast_in, cast_out, cast_shapes = _cast_plan(cast, _cast_step_of(b * tiles_per_seq,
                                                                    tiles_per_seq))
    outs = pl.pallas_call(
        _gmlp_kernel,
        grid=(b, tiles_per_seq),
        in_specs=[
            pl.BlockSpec((1, tm, d), lambda i, j: (i, j, 0)),
            _const_spec((1, d)),
            _const_spec((1, d)),
            _layer_spec((d, dg), 0, col_block=0),
            _layer_spec((d, dg), 0, col_block=1),
            _const_spec((1, dg)),
            _const_spec((1, dg)),
            _layer_spec(ws.shape[1:], layer),
            _layer_spec(bs_full.shape[1:], layer),
            _layer_spec((dg, d), 0),
        ] + cast_in,
        out_specs=[pl.BlockSpec((1, tm, d), lambda i, j: (i, j, 0))] + cast_out,
        out_shape=[jax.ShapeDtypeStruct((b, s, d), F32)] + cast_shapes,
        scratch_shapes=[
            pltpu.VMEM((tm, d), BF16),
            pltpu.VMEM((tm, dg), F32),
            pltpu.VMEM((tm, dg), F32),
            pltpu.VMEM((tm, dg), BF16),
        ],
        compiler_params=_params(("arbitrary", "arbitrary")),
        name="gmlp_mixer",
    )(x, gpre, gpost, w_in, w_in, lng, lnb, ws, bs_full, wout, *[c[0] for c in cast])
    return outs[0], outs[1:]


def _trunk(xs, kt, v, p):
    s, d = xs[0].shape[1:]
    group_rows = [x.shape[0] * s for x in xs]
    b = sum(x.shape[0] for x in xs)
    depth = p["norm_gains"].shape[0]
    ffn_tiles = dict(tm=1024)
    for i in range(depth):
        g = p["norm_gains"][i]
        gain = lambda n: g[n][None, :]
        j = i // 2
        attn_cast = [(p["attn_wq"], i, None), (p["attn_wo"], i, None)]
        if i % 2 == 0:
            gmlp_cast = ([(p["gmlp_w_in"], j, None), (p["gmlp_w_out"], j, None)]
                         if i + 1 < depth else [])
            x, (wq, wo, *gmlp_w) = _pool(xs if i == 0 else (x,), gain(0), gain(1), p["pool_w"],
                                         p["pool_scale"][j][None, :], j, tm=512,
                                         cast=attn_cast + gmlp_cast)
        else:
            x, (wq, wo) = _gmlp(x, gain(0), gain(1), gmlp_w[0], p["gmlp_ln_g"][j][None, :],
                                p["gmlp_ln_b"][j][None, :], p["gmlp_w_s"], p["gmlp_bs_full"],
                                gmlp_w[1], j, tm=512, cast=attn_cast)
        x, ffn_w = _attn(x, gain(2), gain(3), wq, kt, v, wo, i, tm=512,
                         cast=[(p["ffn_w_gate"], i, FFN_CHUNK), (p["ffn_w_up"], i, FFN_CHUNK),
                               (p["ffn_w_down"], i, None)])
        ffn_args = (x.reshape(b * s, d), gain(4), gain(5), *ffn_w, 0)
        if i < depth - 1:
            x = _ffn(*ffn_args, **ffn_tiles).reshape(b, s, d)
    starts = np.cumsum([0] + group_rows[:-1])
    return tuple(_ffn(*ffn_args, **ffn_tiles, row_start=int(r0), n_rows=n).reshape(x_in.shape)
                 for x_in, r0, n in zip(xs, starts, group_rows))


def kernel(x_prompt, x_sample, mem_prompt, mem_sample, norm_gains, mem_norm, pool_w, pool_scale, gmlp_w_in, gmlp_ln_g, gmlp_ln_b, gmlp_w_s, gmlp_b_s, gmlp_w_out, attn_wq, attn_wk, attn_wv, attn_wo, ffn_w_gate, ffn_w_up, ffn_w_down):
    dg = gmlp_w_out.shape[1]
    hd = dg // N_SG_HEADS
    n_layers, n_heads, chunk = gmlp_b_s.shape
    bs_full = jnp.broadcast_to(jnp.swapaxes(gmlp_b_s, 1, 2)[..., None],
                               (n_layers, chunk, n_heads, hd)).reshape(n_layers, chunk, dg)
    p = dict(
        norm_gains=norm_gains,
        pool_w=pool_w.astype(BF16), pool_scale=pool_scale,
        gmlp_w_in=gmlp_w_in, gmlp_w_out=gmlp_w_out,
        gmlp_ln_g=gmlp_ln_g, gmlp_ln_b=gmlp_ln_b, gmlp_w_s=gmlp_w_s.astype(BF16),
        gmlp_bs_full=bs_full,
        attn_wq=attn_wq, attn_wo=attn_wo,
        ffn_w_gate=ffn_w_gate, ffn_w_up=ffn_w_up, ffn_w_down=ffn_w_down,
    )
    wk = attn_wk.astype(BF16)
    wv = attn_wv.astype(BF16)
    kt, v = _kv(jnp.concatenate([mem_prompt, mem_sample], axis=0), mem_norm[:, None, :], wk, wv)
    return _trunk((x_prompt, x_sample), kt, v, p)
```

```python
import functools

import numpy as np
import jax
import jax.numpy as jnp
from jax import lax
from jax.experimental import pallas as pl
from jax.experimental.pallas import tpu as pltpu

F32 = jnp.float32
BF16 = jnp.bfloat16

EPS = 1e-6
POOL_WINDOWS = (2, 4, 8, 16)
POOL_HALO = max(POOL_WINDOWS) // 2
CHUNK = 128
N_SG_HEADS = 8
N_XHEADS = 4

V7X_VMEM_LIMIT_BYTES = 60 * 1024 * 1024
ROW_CHUNK = 16
POOL_ROWS = 64
CAST_STEPS = 32
FFN_CHUNK = 512
TILE_ROWS = dict(pool=512, gmlp=512, attn=512, ffn=1024)

GELU_C0 = np.float32(np.sqrt(2.0 / np.pi))
GELU_C1 = np.float32(0.044715)


def _rms(x, g):
    ms = jnp.mean(x * x, axis=-1, keepdims=True)
    return x * lax.rsqrt(ms + EPS) * g


def _gelu_tanh(x):
    return x * (0.5 * (1.0 + jnp.tanh(GELU_C0 * (x + GELU_C1 * (x * x * x)))))


def _dot(a, b):
    return jnp.dot(a, b, preferred_element_type=F32)


def _row_chunks(n_rows):
    return [slice(r, r + ROW_CHUNK) for r in range(0, n_rows, ROW_CHUNK)]


def _const_spec(shape):
    return pl.BlockSpec(shape, lambda *_: (0,) * len(shape), pipeline_mode=pl.Buffered(1))


def _layer_spec(shape, layer, col_block=0):
    index = (layer,) + (0,) * (len(shape) - 1) + (col_block,)
    return pl.BlockSpec((1,) + tuple(shape), lambda *_: index, pipeline_mode=pl.Buffered(1))


def _cast_plan(cast, step_of):
    in_specs, out_specs, out_shapes = [], [], []
    for w, layer, chunk in cast:
        _, rows, cols = w.shape
        blk_rows = rows // CAST_STEPS
        in_specs.append(pl.BlockSpec((1, blk_rows, cols),
                                     lambda *ids, layer=layer: (layer, step_of(*ids), 0)))
        if chunk is None:
            out_specs.append(pl.BlockSpec((1, blk_rows, cols),
                                          lambda *ids: (0, step_of(*ids), 0)))
            out_shapes.append(jax.ShapeDtypeStruct((1, rows, cols), BF16))
        else:
            out_specs.append(pl.BlockSpec((1, cols // chunk, blk_rows, chunk),
                                          lambda *ids: (0, 0, step_of(*ids), 0)))
            out_shapes.append(jax.ShapeDtypeStruct((1, cols // chunk, rows, chunk), BF16))
    return in_specs, out_specs, out_shapes


def _cast_step_of(n_steps, tiles_per_seq):
    assert n_steps >= CAST_STEPS
    return lambda i, j: jnp.minimum(i * tiles_per_seq + j, CAST_STEPS - 1)


def _convert_blocks(src_refs, dst_refs):
    for src_ref, dst_ref in zip(src_refs, dst_refs):
        if len(dst_ref.shape) == 3:
            dst_ref[...] = src_ref[...].astype(BF16)
        else:
            chunk = dst_ref.shape[3]
            for c in range(dst_ref.shape[1]):
                dst_ref[0, c] = src_ref[0, :, c * chunk:(c + 1) * chunk].astype(BF16)


def _params(semantics):
    return pltpu.CompilerParams(dimension_semantics=semantics,
                                vmem_limit_bytes=V7X_VMEM_LIMIT_BYTES)


def _ffn_kernel(x_ref, gpre_ref, gpost_ref, wg_ref, wu_ref, wd_ref, o_ref, xn_ref):
    k = pl.program_id(1)
    tm = x_ref.shape[0]

    def partial_down_projection():
        xn = xn_ref[...]
        gate = _dot(xn, wg_ref[0, 0])
        up = _dot(xn, wu_ref[0, 0])
        h = (gate * (1.0 / (1.0 + jnp.exp(-gate))) * up).astype(BF16)
        return _dot(h, wd_ref[0])

    @pl.when(k == 0)
    def _():
        gpre = gpre_ref[...]
        for r in _row_chunks(tm):
            xn_ref[r, :] = _rms(x_ref[r, :], gpre).astype(BF16)
        o_ref[...] = partial_down_projection()

    @pl.when(k > 0)
    def _():
        o_ref[...] += partial_down_projection()

    @pl.when(k == pl.num_programs(1) - 1)
    def _():
        gpost = gpost_ref[...]
        for r in _row_chunks(tm):
            o_ref[r, :] = x_ref[r, :] + _rms(o_ref[r, :], gpost)


def _ffn(x, gpre, gpost, wg, wu, wd, layer, *, tm, row_start=0, n_rows=None):
    t, d = x.shape
    t = t if n_rows is None else n_rows
    first_tile = row_start // tm
    n_chunks, tf = wg.shape[1], wg.shape[3]
    return pl.pallas_call(
        _ffn_kernel,
        grid=(t // tm, n_chunks),
        in_specs=[
            pl.BlockSpec((tm, d), lambda i, k: (i + first_tile, 0)),
            pl.BlockSpec((1, d), lambda i, k: (0, 0)),
            pl.BlockSpec((1, d), lambda i, k: (0, 0)),
            pl.BlockSpec((1, 1, d, tf), lambda i, k: (layer, k, 0, 0)),
            pl.BlockSpec((1, 1, d, tf), lambda i, k: (layer, k, 0, 0)),
            pl.BlockSpec((1, tf, d), lambda i, k: (layer, k, 0)),
        ],
        out_specs=pl.BlockSpec((tm, d), lambda i, k: (i, 0)),
        out_shape=jax.ShapeDtypeStruct((t, d), F32),
        scratch_shapes=[pltpu.VMEM((tm, d), BF16)],
        compiler_params=_params(("parallel", "arbitrary")),
        name="ffn",
    )(x, gpre, gpost, wg, wu, wd)


def _kv_kernel(*refs, splits):
    mem_refs = refs[:len(splits)]
    g_ref, wk_ref, wv_ref, kt_ref, v_ref = refs[len(splits):]
    i = pl.program_id(1)

    def project(mem_ref):
        memn = _rms(mem_ref[0], g_ref[0]).astype(BF16)
        kt_ref[0, 0] = _dot(memn, wk_ref[0]).T.astype(BF16)
        v_ref[0, 0] = _dot(memn, wv_ref[0]).astype(BF16)

    for n, mem_ref in enumerate(mem_refs):
        first = splits[n - 1] if n else 0
        pl.when((i >= first) & (i < splits[n]))(functools.partial(project, mem_ref))


def _kv(mems, mem_norm, wk, wv):
    m, d = mems[0].shape[1:]
    n_layers = wk.shape[0]
    splits = tuple(int(c) for c in np.cumsum([mem.shape[0] for mem in mems]))
    b = splits[-1]

    def group_spec(n):
        first = splits[n - 1] if n else 0
        return pl.BlockSpec((1, m, d), lambda l, i: (jnp.clip(i - first, 0, splits[n] - first - 1),
                                                     0, 0))

    return pl.pallas_call(
        functools.partial(_kv_kernel, splits=splits),
        grid=(n_layers, b),
        in_specs=[group_spec(n) for n in range(len(mems))] + [
            pl.BlockSpec((1, 1, d), lambda l, i: (l, 0, 0)),
            pl.BlockSpec((1, d, d), lambda l, i: (l, 0, 0)),
            pl.BlockSpec((1, d, d), lambda l, i: (l, 0, 0)),
        ],
        out_specs=[
            pl.BlockSpec((1, 1, d, m), lambda l, i: (l, i, 0, 0)),
            pl.BlockSpec((1, 1, m, d), lambda l, i: (l, i, 0, 0)),
        ],
        out_shape=[
            jax.ShapeDtypeStruct((n_layers, b, d, m), BF16),
            jax.ShapeDtypeStruct((n_layers, b, m, d), BF16),
        ],
        compiler_params=_params(("arbitrary", "arbitrary")),
        name="memory_kv",
    )(*mems, mem_norm, wk, wv)


def _attn_kernel(x_ref, gpre_ref, gpost_ref, wq_ref, kt_ref, v_ref, wo_ref, *rest):
    n_cast = (len(rest) - 2) // 2
    cast_in, o_ref, cast_out, xn_ref = (rest[:n_cast], rest[n_cast], rest[n_cast + 1:-1],
                                        rest[-1])
    _convert_blocks(cast_in, cast_out)

    tm, d = x_ref.shape[1], x_ref.shape[2]
    hd = d // N_XHEADS
    gpre = gpre_ref[...]
    for r in _row_chunks(tm):
        xn_ref[r, :] = _rms(x_ref[0, r, :], gpre).astype(BF16)
    q = _dot(xn_ref[...], wq_ref[0]).astype(BF16)
    heads = []
    for h in range(N_XHEADS):
        cols = slice(h * hd, (h + 1) * hd)
        s = _dot(q[:, cols], kt_ref[0, 0, cols, :]) * (hd ** -0.5)
        s = s - jnp.max(s, axis=-1, keepdims=True)
        e = jnp.exp(s)
        p = (e / jnp.sum(e, axis=-1, keepdims=True)).astype(BF16)
        heads.append(_dot(p, v_ref[0, 0, :, cols]).astype(BF16))
    o = jnp.concatenate(heads, axis=-1)
    o_ref[0] = _dot(o, wo_ref[0])
    gpost = gpost_ref[...]
    for r in _row_chunks(tm):
        o_ref[0, r, :] = x_ref[0, r, :] + _rms(o_ref[0, r, :], gpost)


def _attn(x, gpre, gpost, wq, kt, v, wo, layer, *, tm, cast=()):
    b, s, d = x.shape
    m = v.shape[2]
    tiles_per_seq = s // tm
    cast_in, cast_out, cast_shapes = _cast_plan(cast, _cast_step_of(b * tiles_per_seq,
                                                                    tiles_per_seq))
    outs = pl.pallas_call(
        _attn_kernel,
        grid=(b, tiles_per_seq),
        in_specs=[
            pl.BlockSpec((1, tm, d), lambda i, j: (i, j, 0)),
            _const_spec((1, d)),
            _const_spec((1, d)),
            _layer_spec((d, d), 0),
            pl.BlockSpec((1, 1, d, m), lambda i, j: (layer, i, 0, 0)),
            pl.BlockSpec((1, 1, m, d), lambda i, j: (layer, i, 0, 0)),
            _layer_spec((d, d), 0),
        ] + cast_in,
        out_specs=[pl.BlockSpec((1, tm, d), lambda i, j: (i, j, 0))] + cast_out,
        out_shape=[jax.ShapeDtypeStruct((b, s, d), F32)] + cast_shapes,
        scratch_shapes=[pltpu.VMEM((tm, d), BF16)],
        compiler_params=_params(("arbitrary", "arbitrary")),
        name="cross_attn",
    )(x, gpre, gpost, wq, kt, v, wo, *[c[0] for c in cast])
    return outs[0], outs[1:]


def _pool_kernel(*refs, seq_len, split, n_cast):
    n_src = 1 if split is None else 2
    sources = [refs[3 * n:3 * n + 3] for n in range(n_src)]
    gpre_ref, gpost_ref, w_ref, scale_ref = refs[3 * n_src:3 * n_src + 4]
    cast_in = refs[3 * n_src + 4:3 * n_src + 4 + n_cast]
    o_ref, cast_out = refs[3 * n_src + 4 + n_cast], refs[3 * n_src + 5 + n_cast:-2]
    ext_ref, diff_ref = refs[-2:]
    _convert_blocks(cast_in, cast_out)
    i = pl.program_id(0)
    j = pl.program_id(1)
    tm, d = o_ref.shape[1], o_ref.shape[2]
    gd = d // len(POOL_WINDOWS)

    def for_active_source(fn):
        if split is None:
            fn(*sources[0])
        else:
            pl.when(i < split)(lambda: fn(*sources[0]))
            pl.when(i >= split)(lambda: fn(*sources[1]))

    def fill_ext(xprev_ref, x_ref, xnext_ref):
        gpre = gpre_ref[...]
        hprev = _rms(xprev_ref[0], gpre)
        hnext = _rms(xnext_ref[0], gpre)
        ext_ref[0:POOL_HALO, :] = jnp.where(j > 0, hprev, 0.0)
        for r in _row_chunks(tm):
            ext_ref[POOL_HALO + r.start:POOL_HALO + r.stop, :] = _rms(x_ref[0, r, :], gpre)
        ext_ref[POOL_HALO + tm:, :] = jnp.where(j < pl.num_programs(1) - 1, hnext, 0.0)

    for_active_source(fill_ext)

    n_ext = POOL_ROWS + 2 * POOL_HALO

    def shift_up(a, k):
        return pltpu.roll(a, n_ext - k, axis=0)

    for r0 in range(0, tm, POOL_ROWS):
        t = j * tm + r0 + lax.broadcasted_iota(jnp.int32, (POOL_ROWS, 1), 0)
        for g, win in enumerate(POOL_WINDOWS):
            half = win // 2
            cols = slice(g * gd, (g + 1) * gd)
            run = ext_ref[r0:r0 + n_ext, cols]
            width = 1
            while width < half:
                run = run + shift_up(run, width)
                width *= 2
            lo = run if half == POOL_HALO else shift_up(run, POOL_HALO - half)
            wsum = lo[:POOL_ROWS] + run[POOL_HALO:POOL_HALO + POOL_ROWS]
            count = (jnp.minimum(t + half, seq_len) - jnp.maximum(t - half, 0)).astype(F32)
            centre = ext_ref[POOL_HALO + r0:POOL_HALO + r0 + POOL_ROWS, cols]
            diff_ref[r0:r0 + POOL_ROWS, cols] = (wsum * (1.0 / count) - centre).astype(BF16)

    for g in range(len(POOL_WINDOWS)):
        cols = slice(g * gd, (g + 1) * gd)
        o_ref[0, :, cols] = _dot(diff_ref[:, cols], w_ref[0, g]) * scale_ref[:, cols]

    def add_residual(xprev_ref, x_ref, xnext_ref):
        gpost = gpost_ref[...]
        for r in _row_chunks(tm):
            o_ref[0, r, :] = x_ref[0, r, :] + _rms(o_ref[0, r, :], gpost)

    for_active_source(add_residual)


def _pool(xs, gpre, gpost, w, scale, layer, *, tm, cast=()):
    s, d = xs[0].shape[1:]
    _, n_groups, gd, _ = w.shape
    n_tiles = s // tm
    halo_blocks_per_tile = tm // POOL_HALO
    n_halo_blocks = s // POOL_HALO
    split = xs[0].shape[0] if len(xs) == 2 else None
    b = sum(x.shape[0] for x in xs)

    def source_specs(n):
        def locate(i, j):
            if split is None:
                return i, j
            if n == 0:
                return jnp.minimum(i, split - 1), jnp.where(i < split, j, n_tiles - 1)
            return jnp.maximum(i - split, 0), jnp.where(i >= split, j, 0)

        def prev_map(i, j):
            bi, tj = locate(i, j)
            return bi, jnp.maximum(tj * halo_blocks_per_tile - 1, 0), 0

        def tile_map(i, j):
            bi, tj = locate(i, j)
            return bi, tj, 0

        def next_map(i, j):
            bi, tj = locate(i, j)
            return bi, jnp.minimum((tj + 1) * halo_blocks_per_tile, n_halo_blocks - 1), 0

        return [pl.BlockSpec((1, POOL_HALO, d), prev_map), pl.BlockSpec((1, tm, d), tile_map),
                pl.BlockSpec((1, POOL_HALO, d), next_map)]

    cast_in, cast_out, cast_shapes = _cast_plan(cast, _cast_step_of(b * n_tiles, n_tiles))
    outs = pl.pallas_call(
        functools.partial(_pool_kernel, seq_len=s, split=split, n_cast=len(cast)),
        grid=(b, n_tiles),
        in_specs=[spec for n in range(len(xs)) for spec in source_specs(n)] + [
            _const_spec((1, d)),
            _const_spec((1, d)),
            _layer_spec((n_groups, gd, gd), layer),
            _const_spec((1, d)),
        ] + cast_in,
        out_specs=[pl.BlockSpec((1, tm, d), lambda i, j: (i, j, 0))] + cast_out,
        out_shape=[jax.ShapeDtypeStruct((b, s, d), F32)] + cast_shapes,
        scratch_shapes=[pltpu.VMEM((tm + 2 * POOL_HALO, d), F32), pltpu.VMEM((tm, d), BF16)],
        compiler_params=_params(("arbitrary", "arbitrary")),
        name="pool_mixer",
    )(*[x for x in xs for _ in range(3)], gpre, gpost, w, scale, *[c[0] for c in cast])
    return outs[0], outs[1:]


def _gmlp_kernel(x_ref, gpre_ref, gpost_ref, wu_ref, wv_ref, lng_ref, lnb_ref, ws_ref, bs_ref,
                 wout_ref, *rest):
    n_cast = (len(rest) - 5) // 2
    cast_in, o_ref, cast_out = rest[:n_cast], rest[n_cast], rest[n_cast + 1:-4]
    xn_ref, u_ref, v_ref, vn_ref = rest[-4:]
    _convert_blocks(cast_in, cast_out)
    tm, d = x_ref.shape[1], x_ref.shape[2]
    dg = u_ref.shape[1]
    hd = dg // N_SG_HEADS
    mix_ref, gated_ref = v_ref, xn_ref
    gpre = gpre_ref[...]
    for r in _row_chunks(tm):
        xn_ref[r, :] = _rms(x_ref[0, r, :], gpre).astype(BF16)

    v_ref[...] = _dot(xn_ref[...], wv_ref[0])
    u_ref[...] = _dot(xn_ref[...], wu_ref[0])
    lng, lnb = lng_ref[...], lnb_ref[...]
    for r in _row_chunks(tm):
        v = _gelu_tanh(v_ref[r, :])
        vc = v - jnp.mean(v, axis=-1, keepdims=True)
        var = jnp.mean(vc * vc, axis=-1, keepdims=True)
        vn_ref[r, :] = (vc * lax.rsqrt(var + EPS) * lng + lnb).astype(BF16)

    for c in range(tm // CHUNK):
        rows = slice(c * CHUNK, (c + 1) * CHUNK)
        for h in range(N_SG_HEADS):
            cols = slice(h * hd, (h + 1) * hd)
            mix_ref[rows, cols] = _dot(ws_ref[0, h], vn_ref[rows, cols])

    for r in _row_chunks(tm):
        bias = bs_ref[0, r.start % CHUNK:r.start % CHUNK + ROW_CHUNK, :]
        gated_ref[r, :] = (_gelu_tanh(u_ref[r, :]) * (mix_ref[r, :] + bias)).astype(BF16)

    o_ref[0] = _dot(gated_ref[...], wout_ref[0])
    gpost = gpost_ref[...]
    for r in _row_chunks(tm):
        o_ref[0, r, :] = x_ref[0, r, :] + _rms(o_ref[0, r, :], gpost)


def _gmlp(x, gpre, gpost, w_in, lng, lnb, ws, bs_full, wout, layer, *, tm, cast=()):
    b, s, d = x.shape
    dg = wout.shape[1]
    assert dg == d
    tiles_per_seq = s // tm
    cast_in, cast_out, cast_shapes = _cast_plan(cast, _cast_step_of(b * tiles_per_seq,
                                                                    tiles_per_seq))
    outs = pl.pallas_call(
        _gmlp_kernel,
        grid=(b, tiles_per_seq),
        in_specs=[
            pl.BlockSpec((1, tm, d), lambda i, j: (i, j, 0)),
            _const_spec((1, d)),
            _const_spec((1, d)),
            _layer_spec((d, dg), 0, col_block=0),
            _layer_spec((d, dg), 0, col_block=1),
            _const_spec((1, dg)),
            _const_spec((1, dg)),
            _layer_spec(ws.shape[1:], layer),
            _layer_spec(bs_full.shape[1:], layer),
            _layer_spec((dg, d), 0),
        ] + cast_in,
        out_specs=[pl.BlockSpec((1, tm, d), lambda i, j: (i, j, 0))] + cast_out,
        out_shape=[jax.ShapeDtypeStruct((b, s, d), F32)] + cast_shapes,
        scratch_shapes=[
            pltpu.VMEM((tm, d), BF16),
            pltpu.VMEM((tm, dg), F32),
            pltpu.VMEM((tm, dg), F32),
            pltpu.VMEM((tm, dg), BF16),
        ],
        compiler_params=_params(("arbitrary", "arbitrary")),
        name="gmlp_mixer",
    )(x, gpre, gpost, w_in, w_in, lng, lnb, ws, bs_full, wout, *[c[0] for c in cast])
    return outs[0], outs[1:]


def _trunk(xs, kt, v, p):
    s, d = xs[0].shape[1:]
    group_rows = [x.shape[0] * s for x in xs]
    b = sum(x.shape[0] for x in xs)
    depth = p["norm_gains"].shape[0]
    ffn_tiles = dict(tm=TILE_ROWS["ffn"])
    for i in range(depth):
        g = p["norm_gains"][i]
        gain = lambda n: g[n][None, :]
        j = i // 2
        attn_cast = [(p["attn_wq"], i, None), (p["attn_wo"], i, None)]
        if i % 2 == 0:
            gmlp_cast = ([(p["gmlp_w_in"], j, None), (p["gmlp_w_out"], j, None)]
                         if i + 1 < depth else [])
            x, (wq, wo, *gmlp_w) = _pool(xs if i == 0 else (x,), gain(0), gain(1), p["pool_w"],
                                         p["pool_scale"][j][None, :], j, tm=TILE_ROWS["pool"],
                                         cast=attn_cast + gmlp_cast)
        else:
            x, (wq, wo) = _gmlp(x, gain(0), gain(1), gmlp_w[0], p["gmlp_ln_g"][j][None, :],
                                p["gmlp_ln_b"][j][None, :], p["gmlp_w_s"], p["gmlp_bs_full"],
                                gmlp_w[1], j, tm=TILE_ROWS["gmlp"], cast=attn_cast)
        x, ffn_w = _attn(x, gain(2), gain(3), wq, kt, v, wo, i, tm=TILE_ROWS["attn"],
                         cast=[(p["ffn_w_gate"], i, FFN_CHUNK), (p["ffn_w_up"], i, FFN_CHUNK),
                               (p["ffn_w_down"], i, None)])
        ffn_args = (x.reshape(b * s, d), gain(4), gain(5), *ffn_w, 0)
        if i < depth - 1:
            x = _ffn(*ffn_args, **ffn_tiles).reshape(b, s, d)
    starts = np.cumsum([0] + group_rows[:-1])
    return tuple(_ffn(*ffn_args, **ffn_tiles, row_start=int(r0), n_rows=n).reshape(x_in.shape)
                 for x_in, r0, n in zip(xs, starts, group_rows))


def kernel(x_prompt, x_sample, mem_prompt, mem_sample, norm_gains, mem_norm, pool_w, pool_scale, gmlp_w_in, gmlp_ln_g, gmlp_ln_b, gmlp_w_s, gmlp_b_s, gmlp_w_out, attn_wq, attn_wk, attn_wv, attn_wo, ffn_w_gate, ffn_w_up, ffn_w_down):
    dg = gmlp_w_out.shape[1]
    hd = dg // N_SG_HEADS
    n_layers, n_heads, chunk = gmlp_b_s.shape
    bs_full = jnp.broadcast_to(jnp.swapaxes(gmlp_b_s, 1, 2)[..., None],
                               (n_layers, chunk, n_heads, hd)).reshape(n_layers, chunk, dg)
    p = dict(
        norm_gains=norm_gains,
        pool_w=pool_w.astype(BF16), pool_scale=pool_scale,
        gmlp_w_in=gmlp_w_in, gmlp_w_out=gmlp_w_out,
        gmlp_ln_g=gmlp_ln_g, gmlp_ln_b=gmlp_ln_b, gmlp_w_s=gmlp_w_s.astype(BF16),
        gmlp_bs_full=bs_full,
        attn_wq=attn_wq, attn_wo=attn_wo,
        ffn_w_gate=ffn_w_gate, ffn_w_up=ffn_w_up, ffn_w_down=ffn_w_down,
    )
    wk = attn_wk.astype(BF16)
    wv = attn_wv.astype(BF16)
    kt, v = _kv((mem_prompt, mem_sample), mem_norm[:, None, :], wk, wv)
    return _trunk((x_prompt, x_sample), kt, v, p)
```

```python
import functools

import numpy as np
import jax
import jax.numpy as jnp
from jax import lax
from jax.experimental import pallas as pl
from jax.experimental.pallas import tpu as pltpu

F32 = jnp.float32
BF16 = jnp.bfloat16

EPS = 1e-6
POOL_WINDOWS = (2, 4, 8, 16)
POOL_HALO = max(POOL_WINDOWS) // 2
CHUNK = 128
N_SG_HEADS = 8
N_XHEADS = 4

V7X_VMEM_LIMIT_BYTES = 60 * 1024 * 1024
ROW_CHUNK = 16
POOL_ROWS = 64
POOL_RING = 3
CAST_STEPS = 32
FFN_CHUNK = 512
FINISH_ROWS = 256
TILE_ROWS = dict(pool=512, gmlp=512, attn=512, ffn=1024)

GELU_C0 = np.float32(np.sqrt(2.0 / np.pi))
GELU_C1 = np.float32(0.044715)


def _rms(x, g):
    ms = jnp.mean(x * x, axis=-1, keepdims=True)
    return x * lax.rsqrt(ms + EPS) * g


def _gelu_tanh(x):
    return x * (0.5 * (1.0 + jnp.tanh(GELU_C0 * (x + GELU_C1 * (x * x * x)))))


def _dot(a, b):
    return jnp.dot(a, b, preferred_element_type=F32)


def _row_chunks(n_rows):
    return [slice(r, r + ROW_CHUNK) for r in range(0, n_rows, ROW_CHUNK)]


def _row_blocks(n_rows):
    return [slice(r, r + FINISH_ROWS) for r in range(0, n_rows, FINISH_ROWS)]


def _row_chunks_of(rows):
    return [slice(rows.start + r.start, rows.start + r.stop)
            for r in _row_chunks(rows.stop - rows.start)]


def _const_spec(shape):
    return pl.BlockSpec(shape, lambda *_: (0,) * len(shape), pipeline_mode=pl.Buffered(1))


def _layer_spec(shape, layer, col_block=0):
    index = (layer,) + (0,) * (len(shape) - 1) + (col_block,)
    return pl.BlockSpec((1,) + tuple(shape), lambda *_: index, pipeline_mode=pl.Buffered(1))


def _cast_plan(cast, step_of):
    in_specs, out_specs, out_shapes = [], [], []
    for w, layer, chunk in cast:
        _, rows, cols = w.shape
        blk_rows = rows // CAST_STEPS
        in_specs.append(pl.BlockSpec((1, blk_rows, cols),
                                     lambda *ids, layer=layer: (layer, step_of(*ids), 0)))
        if chunk is None:
            out_specs.append(pl.BlockSpec((1, blk_rows, cols),
                                          lambda *ids: (0, step_of(*ids), 0)))
            out_shapes.append(jax.ShapeDtypeStruct((1, rows, cols), BF16))
        else:
            out_specs.append(pl.BlockSpec((1, cols // chunk, blk_rows, chunk),
                                          lambda *ids: (0, 0, step_of(*ids), 0)))
            out_shapes.append(jax.ShapeDtypeStruct((1, cols // chunk, rows, chunk), BF16))
    return in_specs, out_specs, out_shapes


def _cast_step_of(n_steps, tiles_per_seq):
    assert n_steps >= CAST_STEPS
    return lambda i, j: jnp.minimum(i * tiles_per_seq + j, CAST_STEPS - 1)


def _convert_blocks(src_refs, dst_refs):
    for src_ref, dst_ref in zip(src_refs, dst_refs):
        if len(dst_ref.shape) == 3:
            dst_ref[...] = src_ref[...].astype(BF16)
        else:
            chunk = dst_ref.shape[3]
            for c in range(dst_ref.shape[1]):
                dst_ref[0, c] = src_ref[0, :, c * chunk:(c + 1) * chunk].astype(BF16)


def _params(semantics):
    return pltpu.CompilerParams(dimension_semantics=semantics,
                                vmem_limit_bytes=V7X_VMEM_LIMIT_BYTES)


def _ffn_kernel(x_ref, gpre_ref, gpost_ref, wg_ref, wu_ref, wd_ref, o_ref, xn_ref):
    k = pl.program_id(1)
    tm = x_ref.shape[0]

    last = pl.num_programs(1) - 1

    def gated_hidden():
        xn = xn_ref[...]
        gate = _dot(xn, wg_ref[0, 0])
        up = _dot(xn, wu_ref[0, 0])
        return (gate * (1.0 / (1.0 + jnp.exp(-gate))) * up).astype(BF16)

    @pl.when(k == 0)
    def _():
        gpre = gpre_ref[...]
        for r in _row_chunks(tm):
            xn_ref[r, :] = _rms(x_ref[r, :], gpre).astype(BF16)
        o_ref[...] = _dot(gated_hidden(), wd_ref[0])

    @pl.when((k > 0) & (k < last))
    def _():
        o_ref[...] += _dot(gated_hidden(), wd_ref[0])

    @pl.when(k == last)
    def _():
        gpost = gpost_ref[...]
        h = gated_hidden()
        for rows in _row_blocks(tm):
            o_ref[rows, :] += _dot(h[rows, :], wd_ref[0])
            for r in _row_chunks_of(rows):
                o_ref[r, :] = x_ref[r, :] + _rms(o_ref[r, :], gpost)


def _ffn(x, gpre, gpost, wg, wu, wd, layer, *, tm, row_start=0, n_rows=None):
    t, d = x.shape
    t = t if n_rows is None else n_rows
    first_tile = row_start // tm
    n_chunks, tf = wg.shape[1], wg.shape[3]
    return pl.pallas_call(
        _ffn_kernel,
        grid=(t // tm, n_chunks),
        in_specs=[
            pl.BlockSpec((tm, d), lambda i, k: (i + first_tile, 0)),
            pl.BlockSpec((1, d), lambda i, k: (0, 0)),
            pl.BlockSpec((1, d), lambda i, k: (0, 0)),
            pl.BlockSpec((1, 1, d, tf), lambda i, k: (layer, k, 0, 0)),
            pl.BlockSpec((1, 1, d, tf), lambda i, k: (layer, k, 0, 0)),
            pl.BlockSpec((1, tf, d), lambda i, k: (layer, k, 0)),
        ],
        out_specs=pl.BlockSpec((tm, d), lambda i, k: (i, 0)),
        out_shape=jax.ShapeDtypeStruct((t, d), F32),
        scratch_shapes=[pltpu.VMEM((tm, d), BF16)],
        compiler_params=_params(("parallel", "arbitrary")),
        name="ffn",
    )(x, gpre, gpost, wg, wu, wd)


def _kv_kernel(*refs, splits):
    mem_refs = refs[:len(splits)]
    g_ref, wk_ref, wv_ref, kt_ref, v_ref = refs[len(splits):]
    i = pl.program_id(1)

    def project(mem_ref):
        memn = _rms(mem_ref[0], g_ref[0]).astype(BF16)
        kt_ref[0, 0] = _dot(memn, wk_ref[0]).T.astype(BF16)
        v_ref[0, 0] = _dot(memn, wv_ref[0]).astype(BF16)

    for n, mem_ref in enumerate(mem_refs):
        first = splits[n - 1] if n else 0
        pl.when((i >= first) & (i < splits[n]))(functools.partial(project, mem_ref))


def _kv(mems, mem_norm, wk, wv):
    m, d = mems[0].shape[1:]
    n_layers = wk.shape[0]
    splits = tuple(int(c) for c in np.cumsum([mem.shape[0] for mem in mems]))
    b = splits[-1]

    def group_spec(n):
        first = splits[n - 1] if n else 0
        return pl.BlockSpec((1, m, d), lambda l, i: (jnp.clip(i - first, 0, splits[n] - first - 1),
                                                     0, 0))

    return pl.pallas_call(
        functools.partial(_kv_kernel, splits=splits),
        grid=(n_layers, b),
        in_specs=[group_spec(n) for n in range(len(mems))] + [
            pl.BlockSpec((1, 1, d), lambda l, i: (l, 0, 0)),
            pl.BlockSpec((1, d, d), lambda l, i: (l, 0, 0)),
            pl.BlockSpec((1, d, d), lambda l, i: (l, 0, 0)),
        ],
        out_specs=[
            pl.BlockSpec((1, 1, d, m), lambda l, i: (l, i, 0, 0)),
            pl.BlockSpec((1, 1, m, d), lambda l, i: (l, i, 0, 0)),
        ],
        out_shape=[
            jax.ShapeDtypeStruct((n_layers, b, d, m), BF16),
            jax.ShapeDtypeStruct((n_layers, b, m, d), BF16),
        ],
        compiler_params=_params(("arbitrary", "arbitrary")),
        name="memory_kv",
    )(*mems, mem_norm, wk, wv)


def _attn_kernel(x_ref, gpre_ref, gpost_ref, wq_ref, kt_ref, v_ref, wo_ref, *rest):
    n_cast = (len(rest) - 2) // 2
    cast_in, o_ref, cast_out, xn_ref = (rest[:n_cast], rest[n_cast], rest[n_cast + 1:-1],
                                        rest[-1])
    _convert_blocks(cast_in, cast_out)

    tm, d = x_ref.shape[1], x_ref.shape[2]
    hd = d // N_XHEADS
    gpre = gpre_ref[...]
    for r in _row_chunks(tm):
        xn_ref[r, :] = _rms(x_ref[0, r, :], gpre).astype(BF16)
    q = _dot(xn_ref[...], wq_ref[0]).astype(BF16)
    heads = []
    for h in range(N_XHEADS):
        cols = slice(h * hd, (h + 1) * hd)
        s = _dot(q[:, cols], kt_ref[0, 0, cols, :]) * (hd ** -0.5)
        s = s - jnp.max(s, axis=-1, keepdims=True)
        e = jnp.exp(s)
        p = (e / jnp.sum(e, axis=-1, keepdims=True)).astype(BF16)
        heads.append(_dot(p, v_ref[0, 0, :, cols]).astype(BF16))
    o = jnp.concatenate(heads, axis=-1)
    o_ref[0] = _dot(o, wo_ref[0])
    gpost = gpost_ref[...]
    for r in _row_chunks(tm):
        o_ref[0, r, :] = x_ref[0, r, :] + _rms(o_ref[0, r, :], gpost)


def _attn(x, gpre, gpost, wq, kt, v, wo, layer, *, tm, cast=()):
    b, s, d = x.shape
    m = v.shape[2]
    tiles_per_seq = s // tm
    cast_in, cast_out, cast_shapes = _cast_plan(cast, _cast_step_of(b * tiles_per_seq,
                                                                    tiles_per_seq))
    outs = pl.pallas_call(
        _attn_kernel,
        grid=(b, tiles_per_seq),
        in_specs=[
            pl.BlockSpec((1, tm, d), lambda i, j: (i, j, 0)),
            _const_spec((1, d)),
            _const_spec((1, d)),
            _layer_spec((d, d), 0),
            pl.BlockSpec((1, 1, d, m), lambda i, j: (layer, i, 0, 0)),
            pl.BlockSpec((1, 1, m, d), lambda i, j: (layer, i, 0, 0)),
            _layer_spec((d, d), 0),
        ] + cast_in,
        out_specs=[pl.BlockSpec((1, tm, d), lambda i, j: (i, j, 0))] + cast_out,
        out_shape=[jax.ShapeDtypeStruct((b, s, d), F32)] + cast_shapes,
        scratch_shapes=[pltpu.VMEM((tm, d), BF16)],
        compiler_params=_params(("arbitrary", "arbitrary")),
        name="cross_attn",
    )(x, gpre, gpost, wq, kt, v, wo, *[c[0] for c in cast])
    return outs[0], outs[1:]


def _pool_kernel(*refs, seq_len, split, n_cast):
    n_src = 1 if split is None else 2
    sources = [refs[3 * n:3 * n + 3] for n in range(n_src)]
    gpre_ref, gpost_ref, w_ref, scale_ref = refs[3 * n_src:3 * n_src + 4]
    cast_in = refs[3 * n_src + 4:3 * n_src + 4 + n_cast]
    o_ref, cast_out = refs[3 * n_src + 4 + n_cast], refs[3 * n_src + 5 + n_cast:-4]
    ext_ref, diff_ref, ring_ref, ring_sems = refs[-4:]
    _convert_blocks(cast_in, cast_out)
    i = pl.program_id(0)
    j = pl.program_id(1)
    tm, d = o_ref.shape[1], o_ref.shape[2]
    gd = d // len(POOL_WINDOWS)
    n_tiles = pl.num_programs(1)
    n_steps = pl.num_programs(0) * n_tiles
    step = i * n_tiles + j

    def tile_copy(src_hbm, batch, tile, slot):
        return pltpu.make_async_copy(src_hbm.at[batch, pl.ds(tile * tm, tm), :],
                                     ring_ref.at[slot], ring_sems.at[slot])

    def request(u):
        batch, tile, slot = u // n_tiles, u % n_tiles, u % POOL_RING
        if split is None:
            tile_copy(sources[0][1], batch, tile, slot).start()
        else:
            pl.when(batch < split)(
                lambda: tile_copy(sources[0][1], batch, tile, slot).start())
            pl.when(batch >= split)(
                lambda: tile_copy(sources[1][1], batch - split, tile, slot).start())

    @pl.when(step == 0)
    def _():
        for u in range(POOL_RING - 1):
            request(u)

    @pl.when(step + POOL_RING - 1 < n_steps)
    def _():
        request(step + POOL_RING - 1)

    slot = step % POOL_RING
    tile_copy(sources[0][1], 0, 0, slot).wait()
    x_ref = ring_ref.at[slot]

    def for_active_source(fn):
        if split is None:
            fn(*sources[0])
        else:
            pl.when(i < split)(lambda: fn(*sources[0]))
            pl.when(i >= split)(lambda: fn(*sources[1]))

    def fill_halo(xprev_ref, _, xnext_ref):
        gpre = gpre_ref[...]
        hprev = _rms(xprev_ref[0], gpre)
        hnext = _rms(xnext_ref[0], gpre)
        ext_ref[0:POOL_HALO, :] = jnp.where(j > 0, hprev, 0.0)
        ext_ref[POOL_HALO + tm:, :] = jnp.where(j < pl.num_programs(1) - 1, hnext, 0.0)

    for_active_source(fill_halo)
    gpre = gpre_ref[...]
    for r in _row_chunks(tm):
        ext_ref[POOL_HALO + r.start:POOL_HALO + r.stop, :] = _rms(x_ref[r, :], gpre)

    n_ext = POOL_ROWS + 2 * POOL_HALO

    def shift_up(a, k):
        return pltpu.roll(a, n_ext - k, axis=0)

    for r0 in range(0, tm, POOL_ROWS):
        t = j * tm + r0 + lax.broadcasted_iota(jnp.int32, (POOL_ROWS, 1), 0)
        for g, win in enumerate(POOL_WINDOWS):
            half = win // 2
            cols = slice(g * gd, (g + 1) * gd)
            run = ext_ref[r0:r0 + n_ext, cols]
            width = 1
            while width < half:
                run = run + shift_up(run, width)
                width *= 2
            lo = run if half == POOL_HALO else shift_up(run, POOL_HALO - half)
            wsum = lo[:POOL_ROWS] + run[POOL_HALO:POOL_HALO + POOL_ROWS]
            count = (jnp.minimum(t + half, seq_len) - jnp.maximum(t - half, 0)).astype(F32)
            centre = ext_ref[POOL_HALO + r0:POOL_HALO + r0 + POOL_ROWS, cols]
            diff_ref[r0:r0 + POOL_ROWS, cols] = (wsum * (1.0 / count) - centre).astype(BF16)

    for g in range(len(POOL_WINDOWS)):
        cols = slice(g * gd, (g + 1) * gd)
        o_ref[0, :, cols] = _dot(diff_ref[:, cols], w_ref[0, g]) * scale_ref[:, cols]

    gpost = gpost_ref[...]
    for r in _row_chunks(tm):
        o_ref[0, r, :] = x_ref[r, :] + _rms(o_ref[0, r, :], gpost)


def _pool(xs, gpre, gpost, w, scale, layer, *, tm, cast=()):
    s, d = xs[0].shape[1:]
    _, n_groups, gd, _ = w.shape
    n_tiles = s // tm
    halo_blocks_per_tile = tm // POOL_HALO
    n_halo_blocks = s // POOL_HALO
    split = xs[0].shape[0] if len(xs) == 2 else None
    b = sum(x.shape[0] for x in xs)

    def source_specs(n):
        def locate(i, j):
            if split is None:
                return i, j
            if n == 0:
                return jnp.minimum(i, split - 1), jnp.where(i < split, j, n_tiles - 1)
            return jnp.maximum(i - split, 0), jnp.where(i >= split, j, 0)

        def prev_map(i, j):
            bi, tj = locate(i, j)
            return bi, jnp.maximum(tj * halo_blocks_per_tile - 1, 0), 0

        def next_map(i, j):
            bi, tj = locate(i, j)
            return bi, jnp.minimum((tj + 1) * halo_blocks_per_tile, n_halo_blocks - 1), 0

        return [pl.BlockSpec((1, POOL_HALO, d), prev_map), pl.BlockSpec(memory_space=pl.ANY),
                pl.BlockSpec((1, POOL_HALO, d), next_map)]

    cast_in, cast_out, cast_shapes = _cast_plan(cast, _cast_step_of(b * n_tiles, n_tiles))
    outs = pl.pallas_call(
        functools.partial(_pool_kernel, seq_len=s, split=split, n_cast=len(cast)),
        grid=(b, n_tiles),
        in_specs=[spec for n in range(len(xs)) for spec in source_specs(n)] + [
            _const_spec((1, d)),
            _const_spec((1, d)),
            _layer_spec((n_groups, gd, gd), layer),
            _const_spec((1, d)),
        ] + cast_in,
        out_specs=[pl.BlockSpec((1, tm, d), lambda i, j: (i, j, 0))] + cast_out,
        out_shape=[jax.ShapeDtypeStruct((b, s, d), F32)] + cast_shapes,
        scratch_shapes=[pltpu.VMEM((tm + 2 * POOL_HALO, d), F32), pltpu.VMEM((tm, d), BF16),
                        pltpu.VMEM((POOL_RING, tm, d), F32),
                        pltpu.SemaphoreType.DMA((POOL_RING,))],
        compiler_params=_params(("arbitrary", "arbitrary")),
        name="pool_mixer",
    )(*[x for x in xs for _ in range(3)], gpre, gpost, w, scale, *[c[0] for c in cast])
    return outs[0], outs[1:]


def _gmlp_kernel(x_ref, gpre_ref, gpost_ref, wu_ref, wv_ref, lng_ref, lnb_ref, ws_ref, bs_ref,
                 wout_ref, *rest):
    n_cast = (len(rest) - 5) // 2
    cast_in, o_ref, cast_out = rest[:n_cast], rest[n_cast], rest[n_cast + 1:-4]
    xn_ref, u_ref, v_ref, vn_ref = rest[-4:]
    _convert_blocks(cast_in, cast_out)
    tm, d = x_ref.shape[1], x_ref.shape[2]
    dg = u_ref.shape[1]
    hd = dg // N_SG_HEADS
    mix_ref, gated_ref = v_ref, xn_ref
    gpre = gpre_ref[...]
    for r in _row_chunks(tm):
        xn_ref[r, :] = _rms(x_ref[0, r, :], gpre).astype(BF16)

    v_ref[...] = _dot(xn_ref[...], wv_ref[0])
    u_ref[...] = _dot(xn_ref[...], wu_ref[0])
    lng, lnb = lng_ref[...], lnb_ref[...]
    for r in _row_chunks(tm):
        v = _gelu_tanh(v_ref[r, :])
        vc = v - jnp.mean(v, axis=-1, keepdims=True)
        var = jnp.mean(vc * vc, axis=-1, keepdims=True)
        vn_ref[r, :] = (vc * lax.rsqrt(var + EPS) * lng + lnb).astype(BF16)

    for c in range(tm // CHUNK):
        rows = slice(c * CHUNK, (c + 1) * CHUNK)
        for h in range(N_SG_HEADS):
            cols = slice(h * hd, (h + 1) * hd)
            mix_ref[rows, cols] = _dot(ws_ref[0, h], vn_ref[rows, cols])

    for r in _row_chunks(tm):
        bias = bs_ref[0, r.start % CHUNK:r.start % CHUNK + ROW_CHUNK, :]
        gated_ref[r, :] = (_gelu_tanh(u_ref[r, :]) * (mix_ref[r, :] + bias)).astype(BF16)

    o_ref[0] = _dot(gated_ref[...], wout_ref[0])
    gpost = gpost_ref[...]
    for r in _row_chunks(tm):
        o_ref[0, r, :] = x_ref[0, r, :] + _rms(o_ref[0, r, :], gpost)


def _gmlp(x, gpre, gpost, w_in, lng, lnb, ws, bs_full, wout, layer, *, tm, cast=()):
    b, s, d = x.shape
    dg = wout.shape[1]
    assert dg == d
    tiles_per_seq = s // tm
    cast_in, cast_out, cast_shapes = _cast_plan(cast, _cast_step_of(b * tiles_per_seq,
                                                                    tiles_per_seq))
    outs = pl.pallas_call(
        _gmlp_kernel,
        grid=(b, tiles_per_seq),
        in_specs=[
            pl.BlockSpec((1, tm, d), lambda i, j: (i, j, 0)),
            _const_spec((1, d)),
            _const_spec((1, d)),
            _layer_spec((d, dg), 0, col_block=0),
            _layer_spec((d, dg), 0, col_block=1),
            _const_spec((1, dg)),
            _const_spec((1, dg)),
            _layer_spec(ws.shape[1:], layer),
            _layer_spec(bs_full.shape[1:], layer),
            _layer_spec((dg, d), 0),
        ] + cast_in,
        out_specs=[pl.BlockSpec((1, tm, d), lambda i, j: (i, j, 0))] + cast_out,
        out_shape=[jax.ShapeDtypeStruct((b, s, d), F32)] + cast_shapes,
        scratch_shapes=[
            pltpu.VMEM((tm, d), BF16),
            pltpu.VMEM((tm, dg), F32),
            pltpu.VMEM((tm, dg), F32),
            pltpu.VMEM((tm, dg), BF16),
        ],
        compiler_params=_params(("arbitrary", "arbitrary")),
        name="gmlp_mixer",
    )(x, gpre, gpost, w_in, w_in, lng, lnb, ws, bs_full, wout, *[c[0] for c in cast])
    return outs[0], outs[1:]


def _trunk(xs, kt, v, p):
    s, d = xs[0].shape[1:]
    group_rows = [x.shape[0] * s for x in xs]
    b = sum(x.shape[0] for x in xs)
    depth = p["norm_gains"].shape[0]
    ffn_tiles = dict(tm=TILE_ROWS["ffn"])
    for i in range(depth):
        g = p["norm_gains"][i]
        gain = lambda n: g[n][None, :]
        j = i // 2
        attn_cast = [(p["attn_wq"], i, None), (p["attn_wo"], i, None)]
        if i % 2 == 0:
            gmlp_cast = ([(p["gmlp_w_in"], j, None), (p["gmlp_w_out"], j, None)]
                         if i + 1 < depth else [])
            x, (wq, wo, *gmlp_w) = _pool(xs if i == 0 else (x,), gain(0), gain(1), p["pool_w"],
                                         p["pool_scale"][j][None, :], j, tm=TILE_ROWS["pool"],
                                         cast=attn_cast + gmlp_cast)
        else:
            x, (wq, wo) = _gmlp(x, gain(0), gain(1), gmlp_w[0], p["gmlp_ln_g"][j][None, :],
                                p["gmlp_ln_b"][j][None, :], p["gmlp_w_s"], p["gmlp_bs_full"],
                                gmlp_w[1], j, tm=TILE_ROWS["gmlp"], cast=attn_cast)
        x, ffn_w = _attn(x, gain(2), gain(3), wq, kt, v, wo, i, tm=TILE_ROWS["attn"],
                         cast=[(p["ffn_w_gate"], i, FFN_CHUNK), (p["ffn_w_up"], i, FFN_CHUNK),
                               (p["ffn_w_down"], i, None)])
        ffn_args = (x.reshape(b * s, d), gain(4), gain(5), *ffn_w, 0)
        if i < depth - 1:
            x = _ffn(*ffn_args, **ffn_tiles).reshape(b, s, d)
    starts = np.cumsum([0] + group_rows[:-1])
    return tuple(_ffn(*ffn_args, **ffn_tiles, row_start=int(r0), n_rows=n).reshape(x_in.shape)
                 for x_in, r0, n in zip(xs, starts, group_rows))


def kernel(x_prompt, x_sample, mem_prompt, mem_sample, norm_gains, mem_norm, pool_w, pool_scale, gmlp_w_in, gmlp_ln_g, gmlp_ln_b, gmlp_w_s, gmlp_b_s, gmlp_w_out, attn_wq, attn_wk, attn_wv, attn_wo, ffn_w_gate, ffn_w_up, ffn_w_down):
    dg = gmlp_w_out.shape[1]
    hd = dg // N_SG_HEADS
    n_layers, n_heads, chunk = gmlp_b_s.shape
    bs_full = jnp.broadcast_to(jnp.swapaxes(gmlp_b_s, 1, 2)[..., None],
                               (n_layers, chunk, n_heads, hd)).reshape(n_layers, chunk, dg)
    p = dict(
        norm_gains=norm_gains,
        pool_w=pool_w.astype(BF16), pool_scale=pool_scale,
        gmlp_w_in=gmlp_w_in, gmlp_w_out=gmlp_w_out,
        gmlp_ln_g=gmlp_ln_g, gmlp_ln_b=gmlp_ln_b, gmlp_w_s=gmlp_w_s.astype(BF16),
        gmlp_bs_full=bs_full,
        attn_wq=attn_wq, attn_wo=attn_wo,
        ffn_w_gate=ffn_w_gate, ffn_w_up=ffn_w_up, ffn_w_down=ffn_w_down,
    )
    wk = attn_wk.astype(BF16)
    wv = attn_wv.astype(BF16)
    kt, v = _kv((mem_prompt, mem_sample), mem_norm[:, None, :], wk, wv)
    return _trunk((x_prompt, x_sample), kt, v, p)
```

```python
import functools

import numpy as np
import jax
import jax.numpy as jnp
from jax import lax
from jax.experimental import pallas as pl
from jax.experimental.pallas import tpu as pltpu

F32 = jnp.float32
BF16 = jnp.bfloat16

EPS = 1e-6
POOL_WINDOWS = (2, 4, 8, 16)
POOL_HALO = max(POOL_WINDOWS) // 2
CHUNK = 128
N_SG_HEADS = 8
N_XHEADS = 4

V7X_VMEM_LIMIT_BYTES = 60 * 1024 * 1024
ROW_CHUNK = 16
POOL_ROWS = 64
POOL_RING = 3
CAST_STEPS = 32
FFN_CHUNK = 512
FINISH_ROWS = 256
TILE_ROWS = dict(pool=512, gmlp=512, attn=512, ffn=1024)

GELU_C0 = np.float32(np.sqrt(2.0 / np.pi))
GELU_C1 = np.float32(0.044715)


def _rms(x, g):
    ms = jnp.mean(x * x, axis=-1, keepdims=True)
    return x * lax.rsqrt(ms + EPS) * g


def _gelu_tanh(x):
    return x * (0.5 * (1.0 + jnp.tanh(GELU_C0 * (x + GELU_C1 * (x * x * x)))))


def _dot(a, b):
    return jnp.dot(a, b, preferred_element_type=F32)


def _row_chunks(n_rows):
    return [slice(r, r + ROW_CHUNK) for r in range(0, n_rows, ROW_CHUNK)]


def _row_blocks(n_rows):
    return [slice(r, r + FINISH_ROWS) for r in range(0, n_rows, FINISH_ROWS)]


def _row_chunks_of(rows):
    return [slice(rows.start + r.start, rows.start + r.stop)
            for r in _row_chunks(rows.stop - rows.start)]


def _const_spec(shape):
    return pl.BlockSpec(shape, lambda *_: (0,) * len(shape), pipeline_mode=pl.Buffered(1))


def _layer_spec(shape, layer, col_block=0):
    index = (layer,) + (0,) * (len(shape) - 1) + (col_block,)
    return pl.BlockSpec((1,) + tuple(shape), lambda *_: index, pipeline_mode=pl.Buffered(1))


def _cast_plan(cast, step_of):
    in_specs, out_specs, out_shapes = [], [], []
    for w, layer, chunk in cast:
        _, rows, cols = w.shape
        blk_rows = rows // CAST_STEPS
        in_specs.append(pl.BlockSpec((1, blk_rows, cols),
                                     lambda *ids, layer=layer: (layer, step_of(*ids), 0)))
        if chunk is None:
            out_specs.append(pl.BlockSpec((1, blk_rows, cols),
                                          lambda *ids: (0, step_of(*ids), 0)))
            out_shapes.append(jax.ShapeDtypeStruct((1, rows, cols), BF16))
        else:
            out_specs.append(pl.BlockSpec((1, cols // chunk, blk_rows, chunk),
                                          lambda *ids: (0, 0, step_of(*ids), 0)))
            out_shapes.append(jax.ShapeDtypeStruct((1, cols // chunk, rows, chunk), BF16))
    return in_specs, out_specs, out_shapes


def _cast_step_of(n_steps, tiles_per_seq):
    assert n_steps >= CAST_STEPS
    return lambda i, j: jnp.minimum(i * tiles_per_seq + j, CAST_STEPS - 1)


def _convert_blocks(src_refs, dst_refs):
    for src_ref, dst_ref in zip(src_refs, dst_refs):
        if len(dst_ref.shape) == 3:
            dst_ref[...] = src_ref[...].astype(BF16)
        else:
            chunk = dst_ref.shape[3]
            for c in range(dst_ref.shape[1]):
                dst_ref[0, c] = src_ref[0, :, c * chunk:(c + 1) * chunk].astype(BF16)


def _params(semantics):
    return pltpu.CompilerParams(dimension_semantics=semantics,
                                vmem_limit_bytes=V7X_VMEM_LIMIT_BYTES)


def _ffn_kernel(x_hbm, gpre_ref, gpost_ref, wg_ref, wu_ref, wd_ref, o_ref, xn_ref, xbuf_ref,
                xsems, *, first_tile):
    i = pl.program_id(0)
    k = pl.program_id(1)
    tm = o_ref.shape[0]
    last = pl.num_programs(1) - 1

    def x_copy(tile, slot):
        return pltpu.make_async_copy(x_hbm.at[pl.ds((tile + first_tile) * tm, tm), :],
                                     xbuf_ref.at[slot], xsems.at[slot])

    @pl.when((i == 0) & (k == 0))
    def _():
        x_copy(0, 0).start()

    @pl.when((k == 1) & (i + 1 < pl.num_programs(0)))
    def _():
        x_copy(i + 1, (i + 1) % 2).start()

    @pl.when(k == 0)
    def _():
        x_copy(i, i % 2).wait()

    x_ref = xbuf_ref.at[i % 2]

    def gated_hidden():
        xn = xn_ref[...]
        gate = _dot(xn, wg_ref[0, 0])
        up = _dot(xn, wu_ref[0, 0])
        return (gate * (1.0 / (1.0 + jnp.exp(-gate))) * up).astype(BF16)

    @pl.when(k == 0)
    def _():
        gpre = gpre_ref[...]
        for r in _row_chunks(tm):
            xn_ref[r, :] = _rms(x_ref[r, :], gpre).astype(BF16)
        o_ref[...] = _dot(gated_hidden(), wd_ref[0])

    @pl.when((k > 0) & (k < last))
    def _():
        o_ref[...] += _dot(gated_hidden(), wd_ref[0])

    @pl.when(k == last)
    def _():
        gpost = gpost_ref[...]
        h = gated_hidden()
        for rows in _row_blocks(tm):
            o_ref[rows, :] += _dot(h[rows, :], wd_ref[0])
            for r in _row_chunks_of(rows):
                o_ref[r, :] = x_ref[r, :] + _rms(o_ref[r, :], gpost)


def _ffn(x, gpre, gpost, wg, wu, wd, layer, *, tm, row_start=0, n_rows=None):
    t, d = x.shape
    t = t if n_rows is None else n_rows
    first_tile = row_start // tm
    n_chunks, tf = wg.shape[1], wg.shape[3]
    assert n_chunks >= 2
    return pl.pallas_call(
        functools.partial(_ffn_kernel, first_tile=first_tile),
        grid=(t // tm, n_chunks),
        in_specs=[
            pl.BlockSpec(memory_space=pl.ANY),
            pl.BlockSpec((1, d), lambda i, k: (0, 0)),
            pl.BlockSpec((1, d), lambda i, k: (0, 0)),
            pl.BlockSpec((1, 1, d, tf), lambda i, k: (layer, k, 0, 0)),
            pl.BlockSpec((1, 1, d, tf), lambda i, k: (layer, k, 0, 0)),
            pl.BlockSpec((1, tf, d), lambda i, k: (layer, k, 0)),
        ],
        out_specs=pl.BlockSpec((tm, d), lambda i, k: (i, 0)),
        out_shape=jax.ShapeDtypeStruct((t, d), F32),
        scratch_shapes=[pltpu.VMEM((tm, d), BF16), pltpu.VMEM((2, tm, d), F32),
                        pltpu.SemaphoreType.DMA((2,))],
        compiler_params=_params(("arbitrary", "arbitrary")),
        name="ffn",
    )(x, gpre, gpost, wg, wu, wd)


def _kv_kernel(*refs, splits):
    mem_refs = refs[:len(splits)]
    g_ref, wk_ref, wv_ref, kt_ref, v_ref = refs[len(splits):]
    i = pl.program_id(1)

    def project(mem_ref):
        memn = _rms(mem_ref[0], g_ref[0]).astype(BF16)
        kt_ref[0, 0] = _dot(memn, wk_ref[0]).T.astype(BF16)
        v_ref[0, 0] = _dot(memn, wv_ref[0]).astype(BF16)

    for n, mem_ref in enumerate(mem_refs):
        first = splits[n - 1] if n else 0
        pl.when((i >= first) & (i < splits[n]))(functools.partial(project, mem_ref))


def _kv(mems, mem_norm, wk, wv):
    m, d = mems[0].shape[1:]
    n_layers = wk.shape[0]
    splits = tuple(int(c) for c in np.cumsum([mem.shape[0] for mem in mems]))
    b = splits[-1]

    def group_spec(n):
        first = splits[n - 1] if n else 0
        return pl.BlockSpec((1, m, d), lambda l, i: (jnp.clip(i - first, 0, splits[n] - first - 1),
                                                     0, 0))

    return pl.pallas_call(
        functools.partial(_kv_kernel, splits=splits),
        grid=(n_layers, b),
        in_specs=[group_spec(n) for n in range(len(mems))] + [
            pl.BlockSpec((1, 1, d), lambda l, i: (l, 0, 0)),
            pl.BlockSpec((1, d, d), lambda l, i: (l, 0, 0)),
            pl.BlockSpec((1, d, d), lambda l, i: (l, 0, 0)),
        ],
        out_specs=[
            pl.BlockSpec((1, 1, d, m), lambda l, i: (l, i, 0, 0)),
            pl.BlockSpec((1, 1, m, d), lambda l, i: (l, i, 0, 0)),
        ],
        out_shape=[
            jax.ShapeDtypeStruct((n_layers, b, d, m), BF16),
            jax.ShapeDtypeStruct((n_layers, b, m, d), BF16),
        ],
        compiler_params=_params(("arbitrary", "arbitrary")),
        name="memory_kv",
    )(*mems, mem_norm, wk, wv)


def _attn_kernel(x_ref, gpre_ref, gpost_ref, wq_ref, kt_ref, v_ref, wo_ref, *rest):
    n_cast = (len(rest) - 2) // 2
    cast_in, o_ref, cast_out, xn_ref = (rest[:n_cast], rest[n_cast], rest[n_cast + 1:-1],
                                        rest[-1])
    _convert_blocks(cast_in, cast_out)

    tm, d = x_ref.shape[1], x_ref.shape[2]
    hd = d // N_XHEADS
    gpre = gpre_ref[...]
    for r in _row_chunks(tm):
        xn_ref[r, :] = _rms(x_ref[0, r, :], gpre).astype(BF16)
    q = _dot(xn_ref[...], wq_ref[0]).astype(BF16)
    heads = []
    for h in range(N_XHEADS):
        cols = slice(h * hd, (h + 1) * hd)
        s = _dot(q[:, cols], kt_ref[0, 0, cols, :]) * (hd ** -0.5)
        s = s - jnp.max(s, axis=-1, keepdims=True)
        e = jnp.exp(s)
        p = (e / jnp.sum(e, axis=-1, keepdims=True)).astype(BF16)
        heads.append(_dot(p, v_ref[0, 0, :, cols]).astype(BF16))
    o = jnp.concatenate(heads, axis=-1)
    o_ref[0] = _dot(o, wo_ref[0])
    gpost = gpost_ref[...]
    for r in _row_chunks(tm):
        o_ref[0, r, :] = x_ref[0, r, :] + _rms(o_ref[0, r, :], gpost)


def _attn(x, gpre, gpost, wq, kt, v, wo, layer, *, tm, cast=()):
    b, s, d = x.shape
    m = v.shape[2]
    tiles_per_seq = s // tm
    cast_in, cast_out, cast_shapes = _cast_plan(cast, _cast_step_of(b * tiles_per_seq,
                                                                    tiles_per_seq))
    outs = pl.pallas_call(
        _attn_kernel,
        grid=(b, tiles_per_seq),
        in_specs=[
            pl.BlockSpec((1, tm, d), lambda i, j: (i, j, 0)),
            _const_spec((1, d)),
            _const_spec((1, d)),
            _layer_spec((d, d), 0),
            pl.BlockSpec((1, 1, d, m), lambda i, j: (layer, i, 0, 0)),
            pl.BlockSpec((1, 1, m, d), lambda i, j: (layer, i, 0, 0)),
            _layer_spec((d, d), 0),
        ] + cast_in,
        out_specs=[pl.BlockSpec((1, tm, d), lambda i, j: (i, j, 0))] + cast_out,
        out_shape=[jax.ShapeDtypeStruct((b, s, d), F32)] + cast_shapes,
        scratch_shapes=[pltpu.VMEM((tm, d), BF16)],
        compiler_params=_params(("arbitrary", "arbitrary")),
        name="cross_attn",
    )(x, gpre, gpost, wq, kt, v, wo, *[c[0] for c in cast])
    return outs[0], outs[1:]


def _pool_kernel(*refs, seq_len, split, n_cast):
    n_src = 1 if split is None else 2
    sources = [refs[3 * n:3 * n + 3] for n in range(n_src)]
    gpre_ref, gpost_ref, w_ref, scale_ref = refs[3 * n_src:3 * n_src + 4]
    cast_in = refs[3 * n_src + 4:3 * n_src + 4 + n_cast]
    o_ref, cast_out = refs[3 * n_src + 4 + n_cast], refs[3 * n_src + 5 + n_cast:-4]
    ext_ref, diff_ref, ring_ref, ring_sems = refs[-4:]
    _convert_blocks(cast_in, cast_out)
    i = pl.program_id(0)
    j = pl.program_id(1)
    tm, d = o_ref.shape[1], o_ref.shape[2]
    gd = d // len(POOL_WINDOWS)
    n_tiles = pl.num_programs(1)
    n_steps = pl.num_programs(0) * n_tiles
    step = i * n_tiles + j

    def tile_copy(src_hbm, batch, tile, slot):
        return pltpu.make_async_copy(src_hbm.at[batch, pl.ds(tile * tm, tm), :],
                                     ring_ref.at[slot], ring_sems.at[slot])

    def request(u):
        batch, tile, slot = u // n_tiles, u % n_tiles, u % POOL_RING
        if split is None:
            tile_copy(sources[0][1], batch, tile, slot).start()
        else:
            pl.when(batch < split)(
                lambda: tile_copy(sources[0][1], batch, tile, slot).start())
            pl.when(batch >= split)(
                lambda: tile_copy(sources[1][1], batch - split, tile, slot).start())

    @pl.when(step == 0)
    def _():
        for u in range(POOL_RING - 1):
            request(u)

    @pl.when(step + POOL_RING - 1 < n_steps)
    def _():
        request(step + POOL_RING - 1)

    slot = step % POOL_RING
    tile_copy(sources[0][1], 0, 0, slot).wait()
    x_ref = ring_ref.at[slot]

    def for_active_source(fn):
        if split is None:
            fn(*sources[0])
        else:
            pl.when(i < split)(lambda: fn(*sources[0]))
            pl.when(i >= split)(lambda: fn(*sources[1]))

    def fill_halo(xprev_ref, _, xnext_ref):
        gpre = gpre_ref[...]
        hprev = _rms(xprev_ref[0], gpre)
        hnext = _rms(xnext_ref[0], gpre)
        ext_ref[0:POOL_HALO, :] = jnp.where(j > 0, hprev, 0.0)
        ext_ref[POOL_HALO + tm:, :] = jnp.where(j < pl.num_programs(1) - 1, hnext, 0.0)

    for_active_source(fill_halo)
    gpre = gpre_ref[...]
    for r in _row_chunks(tm):
        ext_ref[POOL_HALO + r.start:POOL_HALO + r.stop, :] = _rms(x_ref[r, :], gpre)

    n_ext = POOL_ROWS + 2 * POOL_HALO

    def shift_up(a, k):
        return pltpu.roll(a, n_ext - k, axis=0)

    for r0 in range(0, tm, POOL_ROWS):
        t = j * tm + r0 + lax.broadcasted_iota(jnp.int32, (POOL_ROWS, 1), 0)
        for g, win in enumerate(POOL_WINDOWS):
            half = win // 2
            cols = slice(g * gd, (g + 1) * gd)
            run = ext_ref[r0:r0 + n_ext, cols]
            width = 1
            while width < half:
                run = run + shift_up(run, width)
                width *= 2
            lo = run if half == POOL_HALO else shift_up(run, POOL_HALO - half)
            wsum = lo[:POOL_ROWS] + run[POOL_HALO:POOL_HALO + POOL_ROWS]
            count = (jnp.minimum(t + half, seq_len) - jnp.maximum(t - half, 0)).astype(F32)
            centre = ext_ref[POOL_HALO + r0:POOL_HALO + r0 + POOL_ROWS, cols]
            diff_ref[r0:r0 + POOL_ROWS, cols] = (wsum * (1.0 / count) - centre).astype(BF16)

    for g in range(len(POOL_WINDOWS)):
        cols = slice(g * gd, (g + 1) * gd)
        o_ref[0, :, cols] = _dot(diff_ref[:, cols], w_ref[0, g]) * scale_ref[:, cols]

    gpost = gpost_ref[...]
    for r in _row_chunks(tm):
        o_ref[0, r, :] = x_ref[r, :] + _rms(o_ref[0, r, :], gpost)


def _pool(xs, gpre, gpost, w, scale, layer, *, tm, cast=()):
    s, d = xs[0].shape[1:]
    _, n_groups, gd, _ = w.shape
    n_tiles = s // tm
    halo_blocks_per_tile = tm // POOL_HALO
    n_halo_blocks = s // POOL_HALO
    split = xs[0].shape[0] if len(xs) == 2 else None
    b = sum(x.shape[0] for x in xs)

    def source_specs(n):
        def locate(i, j):
            if split is None:
                return i, j
            if n == 0:
                return jnp.minimum(i, split - 1), jnp.where(i < split, j, n_tiles - 1)
            return jnp.maximum(i - split, 0), jnp.where(i >= split, j, 0)

        def prev_map(i, j):
            bi, tj = locate(i, j)
            return bi, jnp.maximum(tj * halo_blocks_per_tile - 1, 0), 0

        def next_map(i, j):
            bi, tj = locate(i, j)
            return bi, jnp.minimum((tj + 1) * halo_blocks_per_tile, n_halo_blocks - 1), 0

        return [pl.BlockSpec((1, POOL_HALO, d), prev_map), pl.BlockSpec(memory_space=pl.ANY),
                pl.BlockSpec((1, POOL_HALO, d), next_map)]

    cast_in, cast_out, cast_shapes = _cast_plan(cast, _cast_step_of(b * n_tiles, n_tiles))
    outs = pl.pallas_call(
        functools.partial(_pool_kernel, seq_len=s, split=split, n_cast=len(cast)),
        grid=(b, n_tiles),
        in_specs=[spec for n in range(len(xs)) for spec in source_specs(n)] + [
            _const_spec((1, d)),
            _const_spec((1, d)),
            _layer_spec((n_groups, gd, gd), layer),
            _const_spec((1, d)),
        ] + cast_in,
        out_specs=[pl.BlockSpec((1, tm, d), lambda i, j: (i, j, 0))] + cast_out,
        out_shape=[jax.ShapeDtypeStruct((b, s, d), F32)] + cast_shapes,
        scratch_shapes=[pltpu.VMEM((tm + 2 * POOL_HALO, d), F32), pltpu.VMEM((tm, d), BF16),
                        pltpu.VMEM((POOL_RING, tm, d), F32),
                        pltpu.SemaphoreType.DMA((POOL_RING,))],
        compiler_params=_params(("arbitrary", "arbitrary")),
        name="pool_mixer",
    )(*[x for x in xs for _ in range(3)], gpre, gpost, w, scale, *[c[0] for c in cast])
    return outs[0], outs[1:]


def _gmlp_kernel(x_ref, gpre_ref, gpost_ref, wu_ref, wv_ref, lng_ref, lnb_ref, ws_ref, bs_ref,
                 wout_ref, *rest):
    n_cast = (len(rest) - 5) // 2
    cast_in, o_ref, cast_out = rest[:n_cast], rest[n_cast], rest[n_cast + 1:-4]
    xn_ref, u_ref, v_ref, vn_ref = rest[-4:]
    _convert_blocks(cast_in, cast_out)
    tm, d = x_ref.shape[1], x_ref.shape[2]
    dg = u_ref.shape[1]
    hd = dg // N_SG_HEADS
    mix_ref, gated_ref = v_ref, xn_ref
    gpre = gpre_ref[...]
    for r in _row_chunks(tm):
        xn_ref[r, :] = _rms(x_ref[0, r, :], gpre).astype(BF16)

    v_ref[...] = _dot(xn_ref[...], wv_ref[0])
    u_ref[...] = _dot(xn_ref[...], wu_ref[0])
    lng, lnb = lng_ref[...], lnb_ref[...]
    for r in _row_chunks(tm):
        v = _gelu_tanh(v_ref[r, :])
        vc = v - jnp.mean(v, axis=-1, keepdims=True)
        var = jnp.mean(vc * vc, axis=-1, keepdims=True)
        vn_ref[r, :] = (vc * lax.rsqrt(var + EPS) * lng + lnb).astype(BF16)

    for c in range(tm // CHUNK):
        rows = slice(c * CHUNK, (c + 1) * CHUNK)
        for h in range(N_SG_HEADS):
            cols = slice(h * hd, (h + 1) * hd)
            mix_ref[rows, cols] = _dot(ws_ref[0, h], vn_ref[rows, cols])

    for r in _row_chunks(tm):
        bias = bs_ref[0, r.start % CHUNK:r.start % CHUNK + ROW_CHUNK, :]
        gated_ref[r, :] = (_gelu_tanh(u_ref[r, :]) * (mix_ref[r, :] + bias)).astype(BF16)

    o_ref[0] = _dot(gated_ref[...], wout_ref[0])
    gpost = gpost_ref[...]
    for r in _row_chunks(tm):
        o_ref[0, r, :] = x_ref[0, r, :] + _rms(o_ref[0, r, :], gpost)


def _gmlp(x, gpre, gpost, w_in, lng, lnb, ws, bs_full, wout, layer, *, tm, cast=()):
    b, s, d = x.shape
    dg = wout.shape[1]
    assert dg == d
    tiles_per_seq = s // tm
    cast_in, cast_out, cast_shapes = _cast_plan(cast, _cast_step_of(b * tiles_per_seq,
                                                                    tiles_per_seq))
    outs = pl.pallas_call(
        _gmlp_kernel,
        grid=(b, tiles_per_seq),
        in_specs=[
            pl.BlockSpec((1, tm, d), lambda i, j: (i, j, 0)),
            _const_spec((1, d)),
            _const_spec((1, d)),
            _layer_spec((d, dg), 0, col_block=0),
            _layer_spec((d, dg), 0, col_block=1),
            _const_spec((1, dg)),
            _const_spec((1, dg)),
            _layer_spec(ws.shape[1:], layer),
            _layer_spec(bs_full.shape[1:], layer),
            _layer_spec((dg, d), 0),
        ] + cast_in,
        out_specs=[pl.BlockSpec((1, tm, d), lambda i, j: (i, j, 0))] + cast_out,
        out_shape=[jax.ShapeDtypeStruct((b, s, d), F32)] + cast_shapes,
        scratch_shapes=[
            pltpu.VMEM((tm, d), BF16),
            pltpu.VMEM((tm, dg), F32),
            pltpu.VMEM((tm, dg), F32),
            pltpu.VMEM((tm, dg), BF16),
        ],
        compiler_params=_params(("arbitrary", "arbitrary")),
        name="gmlp_mixer",
    )(x, gpre, gpost, w_in, w_in, lng, lnb, ws, bs_full, wout, *[c[0] for c in cast])
    return outs[0], outs[1:]


def _trunk(xs, kt, v, p):
    s, d = xs[0].shape[1:]
    group_rows = [x.shape[0] * s for x in xs]
    b = sum(x.shape[0] for x in xs)
    depth = p["norm_gains"].shape[0]
    ffn_tiles = dict(tm=TILE_ROWS["ffn"])
    for i in range(depth):
        g = p["norm_gains"][i]
        gain = lambda n: g[n][None, :]
        j = i // 2
        attn_cast = [(p["attn_wq"], i, None), (p["attn_wo"], i, None)]
        if i % 2 == 0:
            gmlp_cast = ([(p["gmlp_w_in"], j, None), (p["gmlp_w_out"], j, None)]
                         if i + 1 < depth else [])
            x, (wq, wo, *gmlp_w) = _pool(xs if i == 0 else (x,), gain(0), gain(1), p["pool_w"],
                                         p["pool_scale"][j][None, :], j, tm=TILE_ROWS["pool"],
                                         cast=attn_cast + gmlp_cast)
        else:
            x, (wq, wo) = _gmlp(x, gain(0), gain(1), gmlp_w[0], p["gmlp_ln_g"][j][None, :],
                                p["gmlp_ln_b"][j][None, :], p["gmlp_w_s"], p["gmlp_bs_full"],
                                gmlp_w[1], j, tm=TILE_ROWS["gmlp"], cast=attn_cast)
        x, ffn_w = _attn(x, gain(2), gain(3), wq, kt, v, wo, i, tm=TILE_ROWS["attn"],
                         cast=[(p["ffn_w_gate"], i, FFN_CHUNK), (p["ffn_w_up"], i, FFN_CHUNK),
                               (p["ffn_w_down"], i, None)])
        ffn_args = (x.reshape(b * s, d), gain(4), gain(5), *ffn_w, 0)
        if i < depth - 1:
            x = _ffn(*ffn_args, **ffn_tiles).reshape(b, s, d)
    starts = np.cumsum([0] + group_rows[:-1])
    return tuple(_ffn(*ffn_args, **ffn_tiles, row_start=int(r0), n_rows=n).reshape(x_in.shape)
                 for x_in, r0, n in zip(xs, starts, group_rows))


def kernel(x_prompt, x_sample, mem_prompt, mem_sample, norm_gains, mem_norm, pool_w, pool_scale, gmlp_w_in, gmlp_ln_g, gmlp_ln_b, gmlp_w_s, gmlp_b_s, gmlp_w_out, attn_wq, attn_wk, attn_wv, attn_wo, ffn_w_gate, ffn_w_up, ffn_w_down):
    dg = gmlp_w_out.shape[1]
    hd = dg // N_SG_HEADS
    n_layers, n_heads, chunk = gmlp_b_s.shape
    bs_full = jnp.broadcast_to(jnp.swapaxes(gmlp_b_s, 1, 2)[..., None],
                               (n_layers, chunk, n_heads, hd)).reshape(n_layers, chunk, dg)
    p = dict(
        norm_gains=norm_gains,
        pool_w=pool_w.astype(BF16), pool_scale=pool_scale,
        gmlp_w_in=gmlp_w_in, gmlp_w_out=gmlp_w_out,
        gmlp_ln_g=gmlp_ln_g, gmlp_ln_b=gmlp_ln_b, gmlp_w_s=gmlp_w_s.astype(BF16),
        gmlp_bs_full=bs_full,
        attn_wq=attn_wq, attn_wo=attn_wo,
        ffn_w_gate=ffn_w_gate, ffn_w_up=ffn_w_up, ffn_w_down=ffn_w_down,
    )
    wk = attn_wk.astype(BF16)
    wv = attn_wv.astype(BF16)
    kt, v = _kv((mem_prompt, mem_sample), mem_norm[:, None, :], wk, wv)
    return _trunk((x_prompt, x_sample), kt, v, p)
```

```python
import functools

import numpy as np
import jax
import jax.numpy as jnp
from jax import lax
from jax.experimental import pallas as pl
from jax.experimental.pallas import tpu as pltpu

F32 = jnp.float32
BF16 = jnp.bfloat16

EPS = 1e-6
POOL_WINDOWS = (2, 4, 8, 16)
POOL_HALO = max(POOL_WINDOWS) // 2
CHUNK = 128
N_SG_HEADS = 8
N_XHEADS = 4

V7X_VMEM_LIMIT_BYTES = 60 * 1024 * 1024
ROW_CHUNK = 16
POOL_ROWS = 64
POOL_RING = 3
CAST_STEPS = 32
FFN_CHUNK = 512
FINISH_ROWS = 256
TILE_ROWS = dict(pool=512, gmlp=512, attn=512, ffn=1024)

GELU_C0 = np.float32(np.sqrt(2.0 / np.pi))
GELU_C1 = np.float32(0.044715)


def _rms(x, g):
    ms = jnp.mean(x * x, axis=-1, keepdims=True)
    return x * lax.rsqrt(ms + EPS) * g


def _gelu_tanh(x):
    return x * (0.5 * (1.0 + jnp.tanh(GELU_C0 * (x + GELU_C1 * (x * x * x)))))


def _dot(a, b):
    return jnp.dot(a, b, preferred_element_type=F32)


def _row_chunks(n_rows):
    return [slice(r, r + ROW_CHUNK) for r in range(0, n_rows, ROW_CHUNK)]


def _row_blocks(n_rows):
    return [slice(r, r + FINISH_ROWS) for r in range(0, n_rows, FINISH_ROWS)]


def _row_chunks_of(rows):
    return [slice(rows.start + r.start, rows.start + r.stop)
            for r in _row_chunks(rows.stop - rows.start)]


def _const_spec(shape):
    return pl.BlockSpec(shape, lambda *_: (0,) * len(shape), pipeline_mode=pl.Buffered(1))


def _layer_spec(shape, layer, col_block=0):
    index = (layer,) + (0,) * (len(shape) - 1) + (col_block,)
    return pl.BlockSpec((1,) + tuple(shape), lambda *_: index, pipeline_mode=pl.Buffered(1))


def _cast_plan(cast, step_of):
    in_specs, out_specs, out_shapes = [], [], []
    for w, layer, chunk in cast:
        _, rows, cols = w.shape
        blk_rows = rows // CAST_STEPS
        in_specs.append(pl.BlockSpec((1, blk_rows, cols),
                                     lambda *ids, layer=layer: (layer, step_of(*ids), 0)))
        if chunk is None:
            out_specs.append(pl.BlockSpec((1, blk_rows, cols),
                                          lambda *ids: (0, step_of(*ids), 0)))
            out_shapes.append(jax.ShapeDtypeStruct((1, rows, cols), BF16))
        else:
            out_specs.append(pl.BlockSpec((1, cols // chunk, blk_rows, chunk),
                                          lambda *ids: (0, 0, step_of(*ids), 0)))
            out_shapes.append(jax.ShapeDtypeStruct((1, cols // chunk, rows, chunk), BF16))
    return in_specs, out_specs, out_shapes


def _cast_step_of(n_steps, tiles_per_seq):
    assert n_steps >= CAST_STEPS
    return lambda i, j: jnp.minimum(i * tiles_per_seq + j, CAST_STEPS - 1)


def _convert_blocks(src_refs, dst_refs):
    for src_ref, dst_ref in zip(src_refs, dst_refs):
        if len(dst_ref.shape) == 3:
            dst_ref[...] = src_ref[...].astype(BF16)
        else:
            chunk = dst_ref.shape[3]
            for c in range(dst_ref.shape[1]):
                dst_ref[0, c] = src_ref[0, :, c * chunk:(c + 1) * chunk].astype(BF16)


def _params(semantics):
    return pltpu.CompilerParams(dimension_semantics=semantics,
                                vmem_limit_bytes=V7X_VMEM_LIMIT_BYTES)


def _ffn_kernel(x_ref, gpre_ref, gpost_ref, wg_ref, wu_ref, wd_ref, o_ref, xn_ref):
    k = pl.program_id(1)
    tm = x_ref.shape[0]

    last = pl.num_programs(1) - 1

    def gated_hidden():
        xn = xn_ref[...]
        gate = _dot(xn, wg_ref[0, 0])
        up = _dot(xn, wu_ref[0, 0])
        return (gate * (1.0 / (1.0 + jnp.exp(-gate))) * up).astype(BF16)

    @pl.when(k == 0)
    def _():
        gpre = gpre_ref[...]
        for r in _row_chunks(tm):
            xn_ref[r, :] = _rms(x_ref[r, :], gpre).astype(BF16)
        o_ref[...] = _dot(gated_hidden(), wd_ref[0])

    @pl.when((k > 0) & (k < last))
    def _():
        o_ref[...] += _dot(gated_hidden(), wd_ref[0])

    @pl.when(k == last)
    def _():
        gpost = gpost_ref[...]
        h = gated_hidden()
        for rows in _row_blocks(tm):
            o_ref[rows, :] += _dot(h[rows, :], wd_ref[0])
            for r in _row_chunks_of(rows):
                o_ref[r, :] = x_ref[r, :] + _rms(o_ref[r, :], gpost)


def _ffn(x, gpre, gpost, wg, wu, wd, layer, *, tm, row_start=0, n_rows=None):
    t, d = x.shape
    t = t if n_rows is None else n_rows
    first_tile = row_start // tm
    n_chunks, tf = wg.shape[1], wg.shape[3]
    return pl.pallas_call(
        _ffn_kernel,
        grid=(t // tm, n_chunks),
        in_specs=[
            pl.BlockSpec((tm, d), lambda i, k: (i + first_tile, 0)),
            pl.BlockSpec((1, d), lambda i, k: (0, 0)),
            pl.BlockSpec((1, d), lambda i, k: (0, 0)),
            pl.BlockSpec((1, 1, d, tf), lambda i, k: (layer, k, 0, 0)),
            pl.BlockSpec((1, 1, d, tf), lambda i, k: (layer, k, 0, 0)),
            pl.BlockSpec((1, tf, d), lambda i, k: (layer, k, 0)),
        ],
        out_specs=pl.BlockSpec((tm, d), lambda i, k: (i, 0)),
        out_shape=jax.ShapeDtypeStruct((t, d), F32),
        scratch_shapes=[pltpu.VMEM((tm, d), BF16)],
        compiler_params=_params(("parallel", "arbitrary")),
        name="ffn",
    )(x, gpre, gpost, wg, wu, wd)


def _kv_kernel(*refs, splits):
    mem_refs = refs[:len(splits)]
    g_ref, wk_ref, wv_ref, kt_ref, v_ref = refs[len(splits):]
    i = pl.program_id(1)

    def project(mem_ref):
        memn = _rms(mem_ref[0], g_ref[0]).astype(BF16)
        kt_ref[0, 0] = _dot(memn, wk_ref[0]).T.astype(BF16)
        v_ref[0, 0] = _dot(memn, wv_ref[0]).astype(BF16)

    for n, mem_ref in enumerate(mem_refs):
        first = splits[n - 1] if n else 0
        pl.when((i >= first) & (i < splits[n]))(functools.partial(project, mem_ref))


def _kv(mems, mem_norm, wk, wv):
    m, d = mems[0].shape[1:]
    n_layers = wk.shape[0]
    splits = tuple(int(c) for c in np.cumsum([mem.shape[0] for mem in mems]))
    b = splits[-1]

    def group_spec(n):
        first = splits[n - 1] if n else 0
        return pl.BlockSpec((1, m, d), lambda l, i: (jnp.clip(i - first, 0, splits[n] - first - 1),
                                                     0, 0))

    return pl.pallas_call(
        functools.partial(_kv_kernel, splits=splits),
        grid=(n_layers, b),
        in_specs=[group_spec(n) for n in range(len(mems))] + [
            pl.BlockSpec((1, 1, d), lambda l, i: (l, 0, 0)),
            pl.BlockSpec((1, d, d), lambda l, i: (l, 0, 0)),
            pl.BlockSpec((1, d, d), lambda l, i: (l, 0, 0)),
        ],
        out_specs=[
            pl.BlockSpec((1, 1, d, m), lambda l, i: (l, i, 0, 0)),
            pl.BlockSpec((1, 1, m, d), lambda l, i: (l, i, 0, 0)),
        ],
        out_shape=[
            jax.ShapeDtypeStruct((n_layers, b, d, m), BF16),
            jax.ShapeDtypeStruct((n_layers, b, m, d), BF16),
        ],
        compiler_params=_params(("arbitrary", "arbitrary")),
        name="memory_kv",
    )(*mems, mem_norm, wk, wv)


def _attn_kernel(x_ref, gpre_ref, gpost_ref, wq_ref, kt_ref, v_ref, wo_ref, *rest):
    n_cast = (len(rest) - 2) // 2
    cast_in, o_ref, cast_out, xn_ref = (rest[:n_cast], rest[n_cast], rest[n_cast + 1:-1],
                                        rest[-1])
    _convert_blocks(cast_in, cast_out)

    tm, d = x_ref.shape[1], x_ref.shape[2]
    hd = d // N_XHEADS
    gpre = gpre_ref[...]
    for r in _row_chunks(tm):
        xn_ref[r, :] = _rms(x_ref[0, r, :], gpre).astype(BF16)
    q = _dot(xn_ref[...], wq_ref[0]).astype(BF16)
    heads = []
    for h in range(N_XHEADS):
        cols = slice(h * hd, (h + 1) * hd)
        s = _dot(q[:, cols], kt_ref[0, 0, cols, :]) * (hd ** -0.5)
        s = s - jnp.max(s, axis=-1, keepdims=True)
        e = jnp.exp(s)
        p = (e / jnp.sum(e, axis=-1, keepdims=True)).astype(BF16)
        heads.append(_dot(p, v_ref[0, 0, :, cols]).astype(BF16))
    o = jnp.concatenate(heads, axis=-1)
    o_ref[0] = _dot(o, wo_ref[0])
    gpost = gpost_ref[...]
    for r in _row_chunks(tm):
        o_ref[0, r, :] = x_ref[0, r, :] + _rms(o_ref[0, r, :], gpost)


def _attn(x, gpre, gpost, wq, kt, v, wo, layer, *, tm, cast=()):
    b, s, d = x.shape
    m = v.shape[2]
    tiles_per_seq = s // tm
    cast_in, cast_out, cast_shapes = _cast_plan(cast, _cast_step_of(b * tiles_per_seq,
                                                                    tiles_per_seq))
    outs = pl.pallas_call(
        _attn_kernel,
        grid=(b, tiles_per_seq),
        in_specs=[
            pl.BlockSpec((1, tm, d), lambda i, j: (i, j, 0)),
            _const_spec((1, d)),
            _const_spec((1, d)),
            _layer_spec((d, d), 0),
            pl.BlockSpec((1, 1, d, m), lambda i, j: (layer, i, 0, 0)),
            pl.BlockSpec((1, 1, m, d), lambda i, j: (layer, i, 0, 0)),
            _layer_spec((d, d), 0),
        ] + cast_in,
        out_specs=[pl.BlockSpec((1, tm, d), lambda i, j: (i, j, 0))] + cast_out,
        out_shape=[jax.ShapeDtypeStruct((b, s, d), F32)] + cast_shapes,
        scratch_shapes=[pltpu.VMEM((tm, d), BF16)],
        compiler_params=_params(("arbitrary", "arbitrary")),
        name="cross_attn",
    )(x, gpre, gpost, wq, kt, v, wo, *[c[0] for c in cast])
    return outs[0], outs[1:]


def _pool_kernel(*refs, seq_len, split, n_cast):
    n_src = 1 if split is None else 2
    sources = [refs[3 * n:3 * n + 3] for n in range(n_src)]
    gpre_ref, gpost_ref, w_ref, scale_ref = refs[3 * n_src:3 * n_src + 4]
    cast_in = refs[3 * n_src + 4:3 * n_src + 4 + n_cast]
    o_ref, cast_out = refs[3 * n_src + 4 + n_cast], refs[3 * n_src + 5 + n_cast:-4]
    ext_ref, diff_ref, ring_ref, ring_sems = refs[-4:]
    _convert_blocks(cast_in, cast_out)
    i = pl.program_id(0)
    j = pl.program_id(1)
    tm, d = o_ref.shape[1], o_ref.shape[2]
    gd = d // len(POOL_WINDOWS)
    n_tiles = pl.num_programs(1)
    n_steps = pl.num_programs(0) * n_tiles
    step = i * n_tiles + j

    def tile_copy(src_hbm, batch, tile, slot):
        return pltpu.make_async_copy(src_hbm.at[batch, pl.ds(tile * tm, tm), :],
                                     ring_ref.at[slot], ring_sems.at[slot])

    def request(u):
        batch, tile, slot = u // n_tiles, u % n_tiles, u % POOL_RING
        if split is None:
            tile_copy(sources[0][1], batch, tile, slot).start(priority=1)
        else:
            pl.when(batch < split)(
                lambda: tile_copy(sources[0][1], batch, tile, slot).start(priority=1))
            pl.when(batch >= split)(
                lambda: tile_copy(sources[1][1], batch - split, tile, slot).start(priority=1))

    @pl.when(step == 0)
    def _():
        for u in range(POOL_RING - 1):
            request(u)

    @pl.when(step + POOL_RING - 1 < n_steps)
    def _():
        request(step + POOL_RING - 1)

    slot = step % POOL_RING
    tile_copy(sources[0][1], 0, 0, slot).wait()
    x_ref = ring_ref.at[slot]

    def for_active_source(fn):
        if split is None:
            fn(*sources[0])
        else:
            pl.when(i < split)(lambda: fn(*sources[0]))
            pl.when(i >= split)(lambda: fn(*sources[1]))

    def fill_halo(xprev_ref, _, xnext_ref):
        gpre = gpre_ref[...]
        hprev = _rms(xprev_ref[0], gpre)
        hnext = _rms(xnext_ref[0], gpre)
        ext_ref[0:POOL_HALO, :] = jnp.where(j > 0, hprev, 0.0)
        ext_ref[POOL_HALO + tm:, :] = jnp.where(j < pl.num_programs(1) - 1, hnext, 0.0)

    for_active_source(fill_halo)
    gpre = gpre_ref[...]
    for r in _row_chunks(tm):
        ext_ref[POOL_HALO + r.start:POOL_HALO + r.stop, :] = _rms(x_ref[r, :], gpre)

    n_ext = POOL_ROWS + 2 * POOL_HALO

    def shift_up(a, k):
        return pltpu.roll(a, n_ext - k, axis=0)

    for r0 in range(0, tm, POOL_ROWS):
        t = j * tm + r0 + lax.broadcasted_iota(jnp.int32, (POOL_ROWS, 1), 0)
        for g, win in enumerate(POOL_WINDOWS):
            half = win // 2
            cols = slice(g * gd, (g + 1) * gd)
            run = ext_ref[r0:r0 + n_ext, cols]
            width = 1
            while width < half:
                run = run + shift_up(run, width)
                width *= 2
            lo = run if half == POOL_HALO else shift_up(run, POOL_HALO - half)
            wsum = lo[:POOL_ROWS] + run[POOL_HALO:POOL_HALO + POOL_ROWS]
            count = (jnp.minimum(t + half, seq_len) - jnp.maximum(t - half, 0)).astype(F32)
            centre = ext_ref[POOL_HALO + r0:POOL_HALO + r0 + POOL_ROWS, cols]
            diff_ref[r0:r0 + POOL_ROWS, cols] = (wsum * (1.0 / count) - centre).astype(BF16)

    for g in range(len(POOL_WINDOWS)):
        cols = slice(g * gd, (g + 1) * gd)
        o_ref[0, :, cols] = _dot(diff_ref[:, cols], w_ref[0, g]) * scale_ref[:, cols]

    gpost = gpost_ref[...]
    for r in _row_chunks(tm):
        o_ref[0, r, :] = x_ref[r, :] + _rms(o_ref[0, r, :], gpost)


def _pool(xs, gpre, gpost, w, scale, layer, *, tm, cast=()):
    s, d = xs[0].shape[1:]
    _, n_groups, gd, _ = w.shape
    n_tiles = s // tm
    halo_blocks_per_tile = tm // POOL_HALO
    n_halo_blocks = s // POOL_HALO
    split = xs[0].shape[0] if len(xs) == 2 else None
    b = sum(x.shape[0] for x in xs)

    def source_specs(n):
        def locate(i, j):
            if split is None:
                return i, j
            if n == 0:
                return jnp.minimum(i, split - 1), jnp.where(i < split, j, n_tiles - 1)
            return jnp.maximum(i - split, 0), jnp.where(i >= split, j, 0)

        def prev_map(i, j):
            bi, tj = locate(i, j)
            return bi, jnp.maximum(tj * halo_blocks_per_tile - 1, 0), 0

        def next_map(i, j):
            bi, tj = locate(i, j)
            return bi, jnp.minimum((tj + 1) * halo_blocks_per_tile, n_halo_blocks - 1), 0

        return [pl.BlockSpec((1, POOL_HALO, d), prev_map), pl.BlockSpec(memory_space=pl.ANY),
                pl.BlockSpec((1, POOL_HALO, d), next_map)]

    cast_in, cast_out, cast_shapes = _cast_plan(cast, _cast_step_of(b * n_tiles, n_tiles))
    outs = pl.pallas_call(
        functools.partial(_pool_kernel, seq_len=s, split=split, n_cast=len(cast)),
        grid=(b, n_tiles),
        in_specs=[spec for n in range(len(xs)) for spec in source_specs(n)] + [
            _const_spec((1, d)),
            _const_spec((1, d)),
            _layer_spec((n_groups, gd, gd), layer),
            _const_spec((1, d)),
        ] + cast_in,
        out_specs=[pl.BlockSpec((1, tm, d), lambda i, j: (i, j, 0))] + cast_out,
        out_shape=[jax.ShapeDtypeStruct((b, s, d), F32)] + cast_shapes,
        scratch_shapes=[pltpu.VMEM((tm + 2 * POOL_HALO, d), F32), pltpu.VMEM((tm, d), BF16),
                        pltpu.VMEM((POOL_RING, tm, d), F32),
                        pltpu.SemaphoreType.DMA((POOL_RING,))],
        compiler_params=_params(("arbitrary", "arbitrary")),
        name="pool_mixer",
    )(*[x for x in xs for _ in range(3)], gpre, gpost, w, scale, *[c[0] for c in cast])
    return outs[0], outs[1:]


def _gmlp_kernel(x_ref, gpre_ref, gpost_ref, wu_ref, wv_ref, lng_ref, lnb_ref, ws_ref, bs_ref,
                 wout_ref, *rest):
    n_cast = (len(rest) - 5) // 2
    cast_in, o_ref, cast_out = rest[:n_cast], rest[n_cast], rest[n_cast + 1:-4]
    xn_ref, u_ref, v_ref, vn_ref = rest[-4:]
    _convert_blocks(cast_in, cast_out)
    tm, d = x_ref.shape[1], x_ref.shape[2]
    dg = u_ref.shape[1]
    hd = dg // N_SG_HEADS
    mix_ref, gated_ref = v_ref, xn_ref
    gpre = gpre_ref[...]
    for r in _row_chunks(tm):
        xn_ref[r, :] = _rms(x_ref[0, r, :], gpre).astype(BF16)

    v_ref[...] = _dot(xn_ref[...], wv_ref[0])
    u_ref[...] = _dot(xn_ref[...], wu_ref[0])
    lng, lnb = lng_ref[...], lnb_ref[...]
    for r in _row_chunks(tm):
        v = _gelu_tanh(v_ref[r, :])
        vc = v - jnp.mean(v, axis=-1, keepdims=True)
        var = jnp.mean(vc * vc, axis=-1, keepdims=True)
        vn_ref[r, :] = (vc * lax.rsqrt(var + EPS) * lng + lnb).astype(BF16)

    for c in range(tm // CHUNK):
        rows = slice(c * CHUNK, (c + 1) * CHUNK)
        for h in range(N_SG_HEADS):
            cols = slice(h * hd, (h + 1) * hd)
            mix_ref[rows, cols] = _dot(ws_ref[0, h], vn_ref[rows, cols])

    for r in _row_chunks(tm):
        bias = bs_ref[0, r.start % CHUNK:r.start % CHUNK + ROW_CHUNK, :]
        gated_ref[r, :] = (_gelu_tanh(u_ref[r, :]) * (mix_ref[r, :] + bias)).astype(BF16)

    o_ref[0] = _dot(gated_ref[...], wout_ref[0])
    gpost = gpost_ref[...]
    for r in _row_chunks(tm):
        o_ref[0, r, :] = x_ref[0, r, :] + _rms(o_ref[0, r, :], gpost)


def _gmlp(x, gpre, gpost, w_in, lng, lnb, ws, bs_full, wout, layer, *, tm, cast=()):
    b, s, d = x.shape
    dg = wout.shape[1]
    assert dg == d
    tiles_per_seq = s // tm
    cast_in, cast_out, cast_shapes = _cast_plan(cast, _cast_step_of(b * tiles_per_seq,
                                                                    tiles_per_seq))
    outs = pl.pallas_call(
        _gmlp_kernel,
        grid=(b, tiles_per_seq),
        in_specs=[
            pl.BlockSpec((1, tm, d), lambda i, j: (i, j, 0)),
            _const_spec((1, d)),
            _const_spec((1, d)),
            _layer_spec((d, dg), 0, col_block=0),
            _layer_spec((d, dg), 0, col_block=1),
            _const_spec((1, dg)),
            _const_spec((1, dg)),
            _layer_spec(ws.shape[1:], layer),
            _layer_spec(bs_full.shape[1:], layer),
            _layer_spec((dg, d), 0),
        ] + cast_in,
        out_specs=[pl.BlockSpec((1, tm, d), lambda i, j: (i, j, 0))] + cast_out,
        out_shape=[jax.ShapeDtypeStruct((b, s, d), F32)] + cast_shapes,
        scratch_shapes=[
            pltpu.VMEM((tm, d), BF16),
            pltpu.VMEM((tm, dg), F32),
            pltpu.VMEM((tm, dg), F32),
            pltpu.VMEM((tm, dg), BF16),
        ],
        compiler_params=_params(("arbitrary", "arbitrary")),
        name="gmlp_mixer",
    )(x, gpre, gpost, w_in, w_in, lng, lnb, ws, bs_full, wout, *[c[0] for c in cast])
    return outs[0], outs[1:]


def _trunk(xs, kt, v, p):
    s, d = xs[0].shape[1:]
    group_rows = [x.shape[0] * s for x in xs]
    b = sum(x.shape[0] for x in xs)
    depth = p["norm_gains"].shape[0]
    ffn_tiles = dict(tm=TILE_ROWS["ffn"])
    for i in range(depth):
        g = p["norm_gains"][i]
        gain = lambda n: g[n][None, :]
        j = i // 2
        attn_cast = [(p["attn_wq"], i, None), (p["attn_wo"], i, None)]
        if i % 2 == 0:
            gmlp_cast = ([(p["gmlp_w_in"], j, None), (p["gmlp_w_out"], j, None)]
                         if i + 1 < depth else [])
            x, (wq, wo, *gmlp_w) = _pool(xs if i == 0 else (x,), gain(0), gain(1), p["pool_w"],
                                         p["pool_scale"][j][None, :], j, tm=TILE_ROWS["pool"],
                                         cast=attn_cast + gmlp_cast)
        else:
            x, (wq, wo) = _gmlp(x, gain(0), gain(1), gmlp_w[0], p["gmlp_ln_g"][j][None, :],
                                p["gmlp_ln_b"][j][None, :], p["gmlp_w_s"], p["gmlp_bs_full"],
                                gmlp_w[1], j, tm=TILE_ROWS["gmlp"], cast=attn_cast)
        x, ffn_w = _attn(x, gain(2), gain(3), wq, kt, v, wo, i, tm=TILE_ROWS["attn"],
                         cast=[(p["ffn_w_gate"], i, FFN_CHUNK), (p["ffn_w_up"], i, FFN_CHUNK),
                               (p["ffn_w_down"], i, None)])
        ffn_args = (x.reshape(b * s, d), gain(4), gain(5), *ffn_w, 0)
        if i < depth - 1:
            x = _ffn(*ffn_args, **ffn_tiles).reshape(b, s, d)
    starts = np.cumsum([0] + group_rows[:-1])
    return tuple(_ffn(*ffn_args, **ffn_tiles, row_start=int(r0), n_rows=n).reshape(x_in.shape)
                 for x_in, r0, n in zip(xs, starts, group_rows))


def kernel(x_prompt, x_sample, mem_prompt, mem_sample, norm_gains, mem_norm, pool_w, pool_scale, gmlp_w_in, gmlp_ln_g, gmlp_ln_b, gmlp_w_s, gmlp_b_s, gmlp_w_out, attn_wq, attn_wk, attn_wv, attn_wo, ffn_w_gate, ffn_w_up, ffn_w_down):
    dg = gmlp_w_out.shape[1]
    hd = dg // N_SG_HEADS
    n_layers, n_heads, chunk = gmlp_b_s.shape
    bs_full = jnp.broadcast_to(jnp.swapaxes(gmlp_b_s, 1, 2)[..., None],
                               (n_layers, chunk, n_heads, hd)).reshape(n_layers, chunk, dg)
    p = dict(
        norm_gains=norm_gains,
        pool_w=pool_w.astype(BF16), pool_scale=pool_scale,
        gmlp_w_in=gmlp_w_in, gmlp_w_out=gmlp_w_out,
        gmlp_ln_g=gmlp_ln_g, gmlp_ln_b=gmlp_ln_b, gmlp_w_s=gmlp_w_s.astype(BF16),
        gmlp_bs_full=bs_full,
        attn_wq=attn_wq, attn_wo=attn_wo,
        ffn_w_gate=ffn_w_gate, ffn_w_up=ffn_w_up, ffn_w_down=ffn_w_down,
    )
    wk = attn_wk.astype(BF16)
    wv = attn_wv.astype(BF16)
    kt, v = _kv((mem_prompt, mem_sample), mem_norm[:, None, :], wk, wv)
    return _trunk((x_prompt, x_sample), kt, v, p)
```
